```python
import math
import jax, jax.numpy as jnp
from jax import lax
import numpy as np

D_MODEL = 1024
BATCH = 8
SEQ = 4096
DEPTH = 4

HEAD_DIM = 64
H_DIL = 6
H_MLA = 6
H_MOBA = 4
D_DIL = H_DIL * HEAD_DIM
MLA_NOPE = 64
MLA_ROPE = 32
MLA_V = 64
MLA_Q_LORA = 384
MLA_KV_LORA = 128
D_MLA = H_MLA * MLA_V
D_MOBA = H_MOBA * HEAD_DIM
D_MIX = D_DIL + D_MLA + D_MOBA
DILATED_PATTERNS = ((128, 1), (512, 4), (2048, 16))
MOBA_BLOCK = 256
MOBA_TOPK = 3
MOBA_QBLK = 64
DENSE_QBLK = 128
NUM_BUCKETS = 32
MAX_EXACT = 16
REL_MAX_DISTANCE = 2048
ROPE_THETA = 10000.0
D_FF = 4 * D_MODEL
EPS = 1e-6
SPLIT_SIZES = (D_DIL, D_DIL, D_DIL, MLA_Q_LORA, MLA_KV_LORA, MLA_ROPE, D_MOBA, D_MOBA, D_MOBA)
D_IN = D_DIL * 3 + MLA_Q_LORA + MLA_KV_LORA + MLA_ROPE + D_MOBA * 3

kernel_name = 'hybrid_dilated_mla_moba_block'


def rmsnorm(x, g):
    xf = x.astype(jnp.float32)
    y = xf * lax.rsqrt(jnp.mean(xf * xf, axis=-1, keepdims=True) + EPS)
    return (y * g.astype(jnp.float32)).astype(x.dtype)


def t5_bucket(dist):
    n = jnp.maximum(dist, 0)
    nf = jnp.maximum(n, 1).astype(jnp.float32)
    large = MAX_EXACT + (jnp.log(nf / MAX_EXACT) / math.log(REL_MAX_DISTANCE / MAX_EXACT)
                         * (NUM_BUCKETS - MAX_EXACT)).astype(jnp.int32)
    large = jnp.minimum(large, NUM_BUCKETS - 1)
    return jnp.where(n < MAX_EXACT, n, large)


def rope(x, cos, sin):
    half = x.shape[-1] // 2
    xf = x.astype(jnp.float32)
    x1, x2 = xf[..., :half], xf[..., half:]
    return jnp.concatenate([x1 * cos - x2 * sin, x2 * cos + x1 * sin], axis=-1).astype(x.dtype)


def banded_attention(q, k, v, bias, span):
    N, H, L, dh = q.shape
    nb = -(-L // span)
    pad = nb * span - L
    padw = ((0, 0), (0, 0), (0, pad), (0, 0))
    qb = jnp.pad(q, padw).reshape(N, H, nb, span, dh)
    kb = jnp.pad(k, padw).reshape(N, H, nb, span, dh)
    vb = jnp.pad(v, padw).reshape(N, H, nb, span, dh)
    shift = ((0, 0), (0, 0), (1, 0), (0, 0), (0, 0))
    kk = jnp.concatenate([jnp.pad(kb, shift)[:, :, :-1], kb], axis=3)
    vv = jnp.concatenate([jnp.pad(vb, shift)[:, :, :-1], vb], axis=3)
    s = jnp.einsum('nhiqd,nhikd->nhiqk', qb, kk).astype(jnp.float32) * (dh ** -0.5)
    s = s + bias[None, :, None].astype(jnp.float32)
    a = jnp.arange(span)[:, None]
    j = jnp.arange(2 * span)[None, :]
    diff = span + a - j
    in_band = (diff >= 0) & (diff <= span)
    has_prev = (jnp.arange(nb) > 0)[:, None, None] | (j >= span)[None]
    mask = in_band[None] & has_prev
    s = jnp.where(mask, s, -jnp.inf)
    lse = jax.nn.logsumexp(s, axis=-1)
    p = jnp.exp(s - lse[..., None]).astype(v.dtype)
    o = jnp.einsum('nhiqk,nhikd->nhiqd', p, vv).reshape(N, H, nb * span, dh)[:, :, :L]
    return o, lse.reshape(N, H, nb * span)[:, :, :L]


def dilated_mixture(q, k, v, bias_tab):
    B, S, H, dh = q.shape
    outs, lses = [], []
    for window, dil in DILATED_PATTERNS:
        L = S // dil
        span = window // dil

        def to_sub(t):
            return t.reshape(B, L, dil, H, dh).transpose(0, 2, 3, 1, 4).reshape(B * dil, H, L, dh)

        diff = span + jnp.arange(span)[:, None] - jnp.arange(2 * span)[None, :]
        bias = bias_tab[t5_bucket(diff * dil)].transpose(2, 0, 1)
        o, lse = banded_attention(to_sub(q), to_sub(k), to_sub(v), bias, span)
        outs.append(o.reshape(B, dil, H, L, dh).transpose(0, 3, 1, 2, 4).reshape(B, S, H, dh))
        lses.append(lse.reshape(B, dil, H, L).transpose(0, 3, 1, 2).reshape(B, S, H))
    wts = jax.nn.softmax(jnp.stack(lses), axis=0)
    out = jnp.sum(wts[..., None] * jnp.stack(outs).astype(jnp.float32), axis=0)
    return out.astype(q.dtype)


def causal_attention(q, k, v, scale):
    B, H, S, dq = q.shape
    nq = S // DENSE_QBLK
    qb = q.reshape(B, H, nq, DENSE_QBLK, dq).transpose(2, 0, 1, 3, 4)
    kpos = jnp.arange(S)

    def blk(args):
        qblk, i = args
        qpos = i * DENSE_QBLK + jnp.arange(DENSE_QBLK)
        s = jnp.einsum('bhqd,bhkd->bhqk', qblk, k).astype(jnp.float32) * scale
        s = jnp.where(kpos[None, :] <= qpos[:, None], s, -jnp.inf)
        p = jax.nn.softmax(s, axis=-1).astype(v.dtype)
        return jnp.einsum('bhqk,bhkd->bhqd', p, v)

    o = lax.map(blk, (qb, jnp.arange(nq)))
    return o.transpose(1, 2, 0, 3, 4).reshape(B, H, S, v.shape[-1])


def mla_attention(c_q, c_kv, k_r, g_q, g_kv, w_uq, w_ukv, cos, sin):
    B, S, _ = c_q.shape
    q = (rmsnorm(c_q, g_q) @ w_uq).reshape(B, S, H_MLA, MLA_NOPE + MLA_ROPE)
    q = jnp.concatenate([q[..., :MLA_NOPE], rope(q[..., MLA_NOPE:], cos, sin)], axis=-1)
    kv = (rmsnorm(c_kv, g_kv) @ w_ukv).reshape(B, S, H_MLA, MLA_NOPE + MLA_V)
    k_rope = jnp.broadcast_to(rope(k_r[:, :, None, :], cos, sin), (B, S, H_MLA, MLA_ROPE))
    k = jnp.concatenate([kv[..., :MLA_NOPE], k_rope], axis=-1)
    v = kv[..., MLA_NOPE:]
    o = causal_attention(q.transpose(0, 2, 1, 3), k.transpose(0, 2, 1, 3), v.transpose(0, 2, 1, 3),
                         (MLA_NOPE + MLA_ROPE) ** -0.5)
    return o.transpose(0, 2, 1, 3).reshape(B, S, D_MLA)


def moba_attention(q, k, v, bias_tab):
    B, H, S, dh = q.shape
    nblk = -(-S // MOBA_BLOCK)
    padw = ((0, 0), (0, 0), (0, nblk * MOBA_BLOCK - S), (0, 0))
    kb = jnp.pad(k, padw).reshape(B, H, nblk, MOBA_BLOCK, dh)
    vb = jnp.pad(v, padw).reshape(B, H, nblk, MOBA_BLOCK, dh)
    kmean = jnp.mean(kb.astype(jnp.float32), axis=3)
    topk = min(MOBA_TOPK, nblk)
    nq = S // MOBA_QBLK
    qc = q.reshape(B, H, nq, MOBA_QBLK, dh).transpose(2, 0, 1, 3, 4)
    bi = jnp.arange(B)[:, None, None, None]
    hi = jnp.arange(H)[None, :, None, None]
    hi5 = jnp.arange(H)[None, :, None, None, None]
    bias_h = bias_tab.T
    scale = dh ** -0.5
    offs = jnp.arange(MOBA_BLOCK)

    def blk(args):
        qblk, c = args
        q0 = c * MOBA_QBLK
        qpos = q0 + jnp.arange(MOBA_QBLK)
        j = q0 // MOBA_BLOCK
        gate = jnp.einsum('bhqd,bhnd->bhqn', qblk.astype(jnp.float32), kmean)
        gate = jnp.where(jnp.arange(nblk) < j, gate, -jnp.inf)
        gval, idx = lax.top_k(gate, topk)
        sel_ok = gval > -jnp.inf
        kg = kb[bi, hi, idx]
        vg = vb[bi, hi, idx]
        s_sel = jnp.einsum('bhqd,bhqnkd->bhqnk', qblk, kg).astype(jnp.float32) * scale
        kpos_sel = idx[..., None] * MOBA_BLOCK + offs
        s_sel = s_sel + bias_h[hi5, t5_bucket(qpos[None, None, :, None, None] - kpos_sel)].astype(jnp.float32)
        s_sel = jnp.where(sel_ok[..., None], s_sel, -jnp.inf).reshape(B, H, MOBA_QBLK, topk * MOBA_BLOCK)
        k_own = lax.dynamic_index_in_dim(kb, j, axis=2, keepdims=False)
        v_own = lax.dynamic_index_in_dim(vb, j, axis=2, keepdims=False)
        rel_own = qpos[:, None] - (j * MOBA_BLOCK + offs)[None, :]
        s_own = jnp.einsum('bhqd,bhkd->bhqk', qblk, k_own).astype(jnp.float32) * scale
        s_own = s_own + bias_h[:, t5_bucket(rel_own)][None].astype(jnp.float32)
        s_own = jnp.where(rel_own >= 0, s_own, -jnp.inf)
        p = jax.nn.softmax(jnp.concatenate([s_own, s_sel], axis=-1), axis=-1).astype(v.dtype)
        p_sel = p[..., MOBA_BLOCK:].reshape(B, H, MOBA_QBLK, topk, MOBA_BLOCK)
        return (jnp.einsum('bhqk,bhkd->bhqd', p[..., :MOBA_BLOCK], v_own)
                + jnp.einsum('bhqnk,bhqnkd->bhqd', p_sel, vg))

    o = lax.map(blk, (qc, jnp.arange(nq)))
    return o.transpose(1, 2, 0, 3, 4).reshape(B, H, S, dh)


def setup_inputs(seed: int = 0) -> dict:
    key = jax.random.key(seed)
    ks = jax.random.split(key, 14)
    nrm = jax.random.normal
    f32 = jnp.float32
    return {
        'x': nrm(ks[0], (BATCH, SEQ, D_MODEL), f32),
        'g_attn': 1.0 + 0.05 * nrm(ks[1], (DEPTH, D_MODEL), f32),
        'w_in': nrm(ks[2], (DEPTH, D_MODEL, D_IN), f32) * D_MODEL ** -0.5,
        'g_q_lora': 1.0 + 0.05 * nrm(ks[3], (DEPTH, MLA_Q_LORA), f32),
        'g_kv_lora': 1.0 + 0.05 * nrm(ks[4], (DEPTH, MLA_KV_LORA), f32),
        'w_uq': nrm(ks[5], (DEPTH, MLA_Q_LORA, H_MLA * (MLA_NOPE + MLA_ROPE)), f32) * MLA_Q_LORA ** -0.5,
        'w_ukv': nrm(ks[6], (DEPTH, MLA_KV_LORA, H_MLA * (MLA_NOPE + MLA_V)), f32) * MLA_KV_LORA ** -0.5,
        'rel_bias': 0.5 * nrm(ks[7], (NUM_BUCKETS, H_DIL + H_MOBA), f32),
        'g_mix': 1.0 + 0.05 * nrm(ks[8], (DEPTH, D_MIX), f32),
        'w_o': nrm(ks[9], (DEPTH, D_MIX, D_MODEL), f32) * D_MIX ** -0.5,
        'g_mlp': 1.0 + 0.05 * nrm(ks[10], (DEPTH, D_MODEL), f32),
        'w_up': nrm(ks[11], (DEPTH, D_MODEL, D_FF), f32) * D_MODEL ** -0.5,
        'w_down': nrm(ks[12], (DEPTH, D_FF, D_MODEL), f32) * D_FF ** -0.5,
        'g_final': 1.0 + 0.05 * nrm(ks[13], (D_MODEL,), f32),
    }


def reference(x, g_attn, w_in, g_q_lora, g_kv_lora, w_uq, w_ukv, rel_bias, g_mix, w_o, g_mlp,
              w_up, w_down, g_final):
    B, S, _ = x.shape
    inv_freq = ROPE_THETA ** (-jnp.arange(0, MLA_ROPE, 2, dtype=jnp.float32) / MLA_ROPE)
    ang = jnp.arange(S, dtype=jnp.float32)[:, None] * inv_freq[None, :]
    cos = jnp.cos(ang)[:, None, :]
    sin = jnp.sin(ang)[:, None, :]
    bias_dil = rel_bias[:, :H_DIL]
    bias_moba = rel_bias[:, H_DIL:]
    points = []
    acc = 0
    for size in SPLIT_SIZES[:-1]:
        acc += size
        points.append(acc)
    for l in range(DEPTH):
        h = rmsnorm(x, g_attn[l])
        proj = h @ w_in[l]
        q_a, k_a, v_a, c_q, c_kv, k_r, q_c, k_c, v_c = jnp.split(proj, points, axis=-1)
        o_a = dilated_mixture(q_a.reshape(B, S, H_DIL, HEAD_DIM), k_a.reshape(B, S, H_DIL, HEAD_DIM),
                              v_a.reshape(B, S, H_DIL, HEAD_DIM), bias_dil).reshape(B, S, D_DIL)
        o_b = mla_attention(c_q, c_kv, k_r, g_q_lora[l], g_kv_lora[l], w_uq[l], w_ukv[l], cos, sin)
        o_c = moba_attention(q_c.reshape(B, S, H_MOBA, HEAD_DIM).transpose(0, 2, 1, 3),
                             k_c.reshape(B, S, H_MOBA, HEAD_DIM).transpose(0, 2, 1, 3),
                             v_c.reshape(B, S, H_MOBA, HEAD_DIM).transpose(0, 2, 1, 3),
                             bias_moba).transpose(0, 2, 1, 3).reshape(B, S, D_MOBA)
        gm = g_mix[l]
        mixed = jnp.concatenate([rmsnorm(o_a, gm[:D_DIL]),
                                 rmsnorm(o_b, gm[D_DIL:D_DIL + D_MLA]),
                                 rmsnorm(o_c, gm[D_DIL + D_MLA:])], axis=-1)
        x = x + mixed @ w_o[l]
        h = rmsnorm(x, g_mlp[l])
        x = x + jnp.square(jax.nn.relu(h @ w_up[l])) @ w_down[l]
    return rmsnorm(x, g_final)
```

```python
import functools
import math

import jax
import jax.numpy as jnp
from jax import lax
from jax.experimental import pallas as pl
from jax.experimental.pallas import tpu as pltpu

LANES = 128
HEAD_DIM = 64
H_DIL, H_MLA, H_MOBA = 6, 6, 4
D_DIL = H_DIL * HEAD_DIM
MLA_NOPE, MLA_ROPE, MLA_V = 64, 32, 64
MLA_Q_LORA, MLA_KV_LORA = 384, 128
D_MLA = H_MLA * MLA_V
D_MOBA = H_MOBA * HEAD_DIM
DILATED_PATTERNS = ((128, 1), (512, 4), (2048, 16))
DIL_SPAN = 128
MOBA_BLOCK = 256
MOBA_TOPK = 3
NUM_BUCKETS = 32
MAX_EXACT = 16
REL_MAX_DISTANCE = 2048
ROPE_THETA = 10000.0
EPS = 1e-6
NEG = -1e30
VMEM_LIMIT = 56 * 1024 * 1024

C_QA, C_KA, C_VA = 0, 384, 768
C_CQ, C_CKV, C_KR = 1152, 1536, 1664
C_QC, C_KC, C_VC = 1792, 2048, 2304
D_IN_PAD = 2560

BF16 = jnp.bfloat16
F32 = jnp.float32


def _bucket(dist):
    n = jnp.maximum(dist, 0)
    nf = jnp.maximum(n, 1).astype(F32)
    large = MAX_EXACT + (jnp.log(nf / MAX_EXACT) / math.log(REL_MAX_DISTANCE / MAX_EXACT)
                         * (NUM_BUCKETS - MAX_EXACT)).astype(jnp.int32)
    large = jnp.minimum(large, NUM_BUCKETS - 1)
    return jnp.where(n < MAX_EXACT, n, large)


def _rms(x, g):
    return x * lax.rsqrt(jnp.mean(x * x, axis=-1, keepdims=True) + EPS) * g


def _dot(a, b):
    return jnp.dot(a, b, preferred_element_type=F32)


def _dot_nt(a, b, precision=None):
    return lax.dot_general(a, b, (((1,), (1,)), ((), ())), preferred_element_type=F32,
                           precision=precision)


def _rope(x, cos, sin_a, sin_b):
    return x * cos + pltpu.roll(x, LANES - 16, 1) * sin_a + pltpu.roll(x, 16, 1) * sin_b


def _in_kernel(x_ref, g_ref, w_ref, gq_ref, gkv_ref, wuq_ref, wk_ref, wv_ref, cos_ref, sa_ref, sb_ref,
               qa_ref, ka_ref, va_ref, qm_ref, km_ref, vm_ref, qc_ref, kc_ref, vc_ref, kmean_ref):
    hb = _rms(x_ref[...], g_ref[...]).astype(BF16)

    def proj(c0, n):
        return _dot(hb, w_ref[:, c0:c0 + n])

    qa_ref[...] = proj(C_QA, D_DIL) * (HEAD_DIM ** -0.5)
    ka_ref[...] = proj(C_KA, D_DIL)
    va_ref[...] = proj(C_VA, D_DIL)

    cos, sin_a, sin_b = cos_ref[...], sa_ref[...], sb_ref[...]
    cq = _rms(proj(C_CQ, MLA_Q_LORA), gq_ref[...]).astype(BF16)
    q = _dot(cq, wuq_ref[...])
    ckv = _rms(proj(C_CKV, MLA_KV_LORA), gkv_ref[...]).astype(BF16)
    kn = _dot(ckv, wk_ref[...])
    vm_ref[...] = _dot(ckv, wv_ref[...]).astype(BF16)
    kr = _rope(proj(C_KR, LANES), cos, sin_a, sin_b)
    scale = (MLA_NOPE + MLA_ROPE) ** -0.5
    for h in range(H_MLA):
        sl = slice(h * LANES, (h + 1) * LANES)
        qm_ref[:, sl] = (_rope(q[:, sl], cos, sin_a, sin_b) * scale).astype(BF16)
        km_ref[:, sl] = (kn[:, sl] + kr).astype(BF16)

    qc_ref[...] = proj(C_QC, D_MOBA) * (HEAD_DIM ** -0.5)
    kc = proj(C_KC, D_MOBA)
    kc_ref[...] = kc.astype(BF16)
    vc_ref[...] = proj(C_VC, D_MOBA).astype(BF16)
    for i in range(kc.shape[0] // MOBA_BLOCK):
        kmean_ref[i] = jnp.mean(kc[i * MOBA_BLOCK:(i + 1) * MOBA_BLOCK], axis=0, keepdims=True)


def _in_proj(x, g, w, gq, gkv, wuq, wk, wv, cos, sin_a, sin_b, seq, tm=512):
    t, d = x.shape
    nt = seq // tm
    row = lambda i: (i, 0)
    const = lambda i: (0, 0)
    pos = lambda i: (i % nt, 0)
    full = lambda a: pl.BlockSpec(a.shape, const)
    outs = [(D_DIL, F32)] * 3 + [(H_MLA * LANES, BF16)] * 2 + [(D_MLA, BF16)] \
        + [(D_MOBA, F32), (D_MOBA, BF16), (D_MOBA, BF16)]
    out_shape = [jax.ShapeDtypeStruct((t, n), dt) for n, dt in outs]
    out_specs = [pl.BlockSpec((tm, n), row) for n, _ in outs]
    out_shape.append(jax.ShapeDtypeStruct((t // MOBA_BLOCK, 1, D_MOBA), F32))
    out_specs.append(pl.BlockSpec((tm // MOBA_BLOCK, 1, D_MOBA), lambda i: (i, 0, 0)))
    return pl.pallas_call(
        _in_kernel,
        grid=(t // tm,),
        in_specs=[pl.BlockSpec((tm, d), row), full(g), full(w), full(gq), full(gkv), full(wuq), full(wk),
                  full(wv), pl.BlockSpec((tm, LANES), pos), pl.BlockSpec((tm, LANES), pos),
                  pl.BlockSpec((tm, LANES), pos)],
        out_specs=out_specs,
        out_shape=out_shape,
        compiler_params=pltpu.CompilerParams(dimension_semantics=("parallel",),
                                             vmem_limit_bytes=VMEM_LIMIT),
        name="in_proj",
    )(x, g, w, gq, gkv, wuq, wk, wv, cos, sin_a, sin_b)


def _dil_kernel(q_ref, k_ref, v_ref, bias_ref, o_ref, oscr, lscr):
    seq = q_ref.shape[0]
    span = DIL_SPAN
    lane = lax.broadcasted_iota(jnp.int32, (span, LANES), 1)
    head0 = lane < HEAD_DIM
    col = lax.broadcasted_iota(jnp.int32, (span, 2 * span), 1)

    for pi, (_, dil) in enumerate(DILATED_PATTERNS):
        nblk = seq // (span * dil)

        def rows(start, dil=dil):
            return pl.ds(start, span) if dil == 1 else pl.ds(start, span, stride=dil)

        def unit(u, carry, pi=pi, dil=dil, rows=rows):
            r = u % dil
            i = u // dil
            cur = i * (span * dil) + r
            prev = jnp.maximum(i - 1, 0) * (span * dil) + r
            q = q_ref[rows(cur), :]
            kk = jnp.concatenate([k_ref[rows(prev), :], k_ref[rows(cur), :]], axis=0).astype(BF16)
            vv = jnp.concatenate([v_ref[rows(prev), :], v_ref[rows(cur), :]], axis=0).astype(BF16)
            has_prev = (col >= span) | (i > 0)
            outs, lses = [], []
            for hh in range(2):
                qh = jnp.where(head0 if hh == 0 else ~head0, q, 0.0).astype(BF16)
                s = _dot_nt(qh, kk) + bias_ref[pi, hh]
                s = jnp.where(has_prev, s, NEG)
                m = jnp.max(s, axis=-1, keepdims=True)
                p = jnp.exp(s - m)
                l = jnp.sum(p, axis=-1, keepdims=True)
                outs.append(_dot(p.astype(BF16), vv) / l)
                lses.append(m + jnp.log(l))
            oscr[pi, rows(cur), :] = jnp.where(head0, outs[0], outs[1])
            lscr[pi, rows(cur), :] = jnp.where(head0, lses[0], lses[1])
            return carry

        lax.fori_loop(0, nblk * dil, unit, 0)

    ct = 512
    for c in range(seq // ct):
        sl = pl.ds(c * ct, ct)
        l0, l1, l2 = lscr[0, sl, :], lscr[1, sl, :], lscr[2, sl, :]
        mx = jnp.maximum(jnp.maximum(l0, l1), l2)
        e0, e1, e2 = jnp.exp(l0 - mx), jnp.exp(l1 - mx), jnp.exp(l2 - mx)
        num = e0 * oscr[0, sl, :] + e1 * oscr[1, sl, :] + e2 * oscr[2, sl, :]
        o_ref[sl, :] = num / (e0 + e1 + e2)


def _dilated(qa, ka, va, bias, batch, seq):
    t = qa.shape[0]
    npair = H_DIL // 2
    blk = pl.BlockSpec((seq, LANES), lambda b, p: (b, p))
    return pl.pallas_call(
        _dil_kernel,
        grid=(batch, npair),
        in_specs=[blk, blk, blk,
                  pl.BlockSpec((len(DILATED_PATTERNS), 2, DIL_SPAN, 2 * DIL_SPAN), lambda b, p: (0, p, 0, 0))],
        out_specs=blk,
        out_shape=jax.ShapeDtypeStruct((t, D_DIL), F32),
        scratch_shapes=[pltpu.VMEM((len(DILATED_PATTERNS), seq, LANES), F32),
                        pltpu.VMEM((len(DILATED_PATTERNS), seq, LANES), F32)],
        compiler_params=pltpu.CompilerParams(dimension_semantics=("parallel", "parallel"),
                                             vmem_limit_bytes=VMEM_LIMIT),
        name="dilated",
    )(qa, ka, va, bias)


def _mla_kernel(q_ref, k_ref, v_ref, o_ref, *, tq):
    i = pl.program_id(2)
    lane = lax.broadcasted_iota(jnp.int32, (tq, LANES), 1)
    rowi = lax.broadcasted_iota(jnp.int32, (tq, tq), 0)
    coli = lax.broadcasted_iota(jnp.int32, (tq, tq), 1)
    causal = coli <= rowi
    outs = []
    for hh in range(2):
        hs = slice(hh * LANES, (hh + 1) * LANES)
        q = q_ref[:, hs]

        def step(j, carry, masked, hs=hs, q=q):
            m, l, acc = carry
            ks = pl.ds(pl.multiple_of(j * tq, tq), tq)
            s = _dot_nt(q, k_ref[ks, hs])
            if masked:
                s = jnp.where(causal, s, NEG)
            m_new = jnp.maximum(m, jnp.max(s, axis=-1, keepdims=True))
            a = jnp.exp(m - m_new)
            p = jnp.exp(s - m_new)
            l = a * l + jnp.sum(p, axis=-1, keepdims=True)
            acc = a * acc + _dot(p.astype(BF16), v_ref[ks, :])
            return m_new, l, acc

        init = (jnp.full((tq, 1), NEG, F32), jnp.zeros((tq, 1), F32), jnp.zeros((tq, LANES), F32))
        carry = lax.fori_loop(0, i, functools.partial(step, masked=False), init)
        m, l, acc = step(i, carry, masked=True)
        outs.append(acc / l)
    o_ref[...] = jnp.where(lane < MLA_V, outs[0], outs[1])


def _mla(qm, km, vm, batch, seq, tq=512):
    t = qm.shape[0]
    nq = seq // tq
    npair = H_MLA // 2
    return pl.pallas_call(
        functools.partial(_mla_kernel, tq=tq),
        grid=(batch, npair, nq),
        in_specs=[pl.BlockSpec((tq, 2 * LANES), lambda b, p, i: (b * nq + i, p)),
                  pl.BlockSpec((seq, 2 * LANES), lambda b, p, i: (b, p)),
                  pl.BlockSpec((seq, LANES), lambda b, p, i: (b, p))],
        out_specs=pl.BlockSpec((tq, LANES), lambda b, p, i: (b * nq + i, p)),
        out_shape=jax.ShapeDtypeStruct((t, D_MLA), F32),
        compiler_params=pltpu.CompilerParams(dimension_semantics=("parallel", "parallel", "arbitrary"),
                                             vmem_limit_bytes=VMEM_LIMIT),
        name="mla",
    )(qm, km, vm)


def _moba_kernel(q_ref, k_ref, v_ref, kmean_ref, bias_ref, o_ref, *, nfar):
    blk = MOBA_BLOCK
    j = pl.program_id(2)
    nblk = kmean_ref.shape[0]
    q = q_ref[...]
    km = kmean_ref[:, 0, :]
    lane = lax.broadcasted_iota(jnp.int32, (blk, LANES), 1)
    head0 = lane < HEAD_DIM
    rowi = lax.broadcasted_iota(jnp.int32, (blk, blk), 0)
    coli = lax.broadcasted_iota(jnp.int32, (blk, blk), 1)
    causal = coli <= rowi
    bidx = lax.broadcasted_iota(jnp.int32, (blk, nblk), 1)
    k_own = k_ref[pl.ds(pl.multiple_of(j * blk, blk), blk), :]
    v_own = v_ref[pl.ds(pl.multiple_of(j * blk, blk), blk), :]
    outs = []
    for hh in range(2):
        qh = jnp.where(head0 if hh == 0 else ~head0, q, 0.0)
        qb = qh.astype(BF16)
        gate = _dot_nt(qh, km, precision=lax.Precision.HIGHEST)
        gate = jnp.where(bidx < j, gate, -jnp.inf)
        sel = jnp.zeros((blk, nblk), F32)
        for _ in range(MOBA_TOPK):
            mx = jnp.max(gate, axis=-1, keepdims=True)
            first = jnp.min(jnp.where(gate == mx, bidx, nblk), axis=-1, keepdims=True)
            hit = bidx == first
            sel = jnp.where(hit & (mx > -jnp.inf), 1.0, sel)
            gate = jnp.where(hit, -jnp.inf, gate)

        s = jnp.where(causal, _dot_nt(qb, k_own) + bias_ref[hh, 0], NEG)
        m = jnp.max(s, axis=-1, keepdims=True)
        p = jnp.exp(s - m)
        l = jnp.sum(p, axis=-1, keepdims=True)
        acc = _dot(p.astype(BF16), v_own)

        def step(n, carry, hh=hh, qb=qb, sel=sel):
            m, l, acc = carry
            ks = pl.ds(pl.multiple_of(n * blk, blk), blk)
            chosen = jnp.sum(jnp.where(bidx == n, sel, 0.0), axis=-1, keepdims=True) > 0.0
            s = _dot_nt(qb, k_ref[ks, :]) + bias_ref[hh, jnp.minimum(j - n, nfar)]
            s = jnp.where(chosen, s, NEG)
            m_new = jnp.maximum(m, jnp.max(s, axis=-1, keepdims=True))
            a = jnp.exp(m - m_new)
            p = jnp.exp(s - m_new)
            l = a * l + jnp.sum(p, axis=-1, keepdims=True)
            acc = a * acc + _dot(p.astype(BF16), v_ref[ks, :])
            return m_new, l, acc

        m, l, acc = lax.fori_loop(0, j, step, (m, l, acc))
        outs.append(acc / l)
    o_ref[...] = jnp.where(head0, outs[0], outs[1])


def _moba(qc, kc, vc, kmean, bias, batch, seq):
    t = qc.shape[0]
    nblk = seq // MOBA_BLOCK
    npair = H_MOBA // 2
    nfar = bias.shape[1] - 1
    return pl.pallas_call(
        functools.partial(_moba_kernel, nfar=nfar),
        grid=(npair, batch, nblk),
        in_specs=[pl.BlockSpec((MOBA_BLOCK, LANES), lambda p, b, j: (b * nblk + j, p)),
                  pl.BlockSpec((seq, LANES), lambda p, b, j: (b, p)),
                  pl.BlockSpec((seq, LANES), lambda p, b, j: (b, p)),
                  pl.BlockSpec((nblk, 1, LANES), lambda p, b, j: (b, 0, p)),
                  pl.BlockSpec((2, nfar + 1, MOBA_BLOCK, MOBA_BLOCK), lambda p, b, j: (p, 0, 0, 0))],
        out_specs=pl.BlockSpec((MOBA_BLOCK, LANES), lambda p, b, j: (b * nblk + j, p)),
        out_shape=jax.ShapeDtypeStruct((t, D_MOBA), F32),
        compiler_params=pltpu.CompilerParams(dimension_semantics=("parallel", "parallel", "arbitrary"),
                                             vmem_limit_bytes=VMEM_LIMIT),
        name="moba",
    )(qc, kc, vc, kmean, bias)


def _post_kernel(x_ref, oa_ref, ob_ref, oc_ref, gmix_ref, wo_ref, gmlp_ref, wup_ref, wdn_ref, gfin_ref,
                 y_ref, *, final, ff_chunk):
    gm = gmix_ref[...]
    x = x_ref[...]
    lo = 0
    for o_ref in (oa_ref, ob_ref, oc_ref):
        n = o_ref.shape[1]
        mixed = _rms(o_ref[...], gm[:, lo:lo + n]).astype(BF16)
        x = x + _dot(mixed, wo_ref[lo:lo + n, :])
        lo += n
    hb = _rms(x, gmlp_ref[...]).astype(BF16)
    y_ref[...] = x
    for c in range(wup_ref.shape[1] // ff_chunk):
        cs = slice(c * ff_chunk, (c + 1) * ff_chunk)
        u = jnp.maximum(_dot(hb, wup_ref[:, cs]), 0.0)
        y_ref[...] += _dot((u * u).astype(BF16), wdn_ref[cs, :])
    if final:
        y_ref[...] = _rms(y_ref[...], gfin_ref[...])


def _post(x, oa, ob, oc, gmix, wo, gmlp, wup, wdn, gfin, final, tm=512, ff_chunk=1024):
    t, d = x.shape
    row = lambda i: (i, 0)
    const = lambda i: (0, 0)
    full = lambda a: pl.BlockSpec(a.shape, const)
    weight = lambda a: pl.BlockSpec(a.shape, const, pipeline_mode=pl.Buffered(1))
    return pl.pallas_call(
        functools.partial(_post_kernel, final=final, ff_chunk=ff_chunk),
        grid=(t // tm,),
        in_specs=[pl.BlockSpec((tm, d), row), pl.BlockSpec((tm, oa.shape[1]), row),
                  pl.BlockSpec((tm, ob.shape[1]), row), pl.BlockSpec((tm, oc.shape[1]), row),
                  full(gmix), weight(wo), full(gmlp), weight(wup), weight(wdn), full(gfin)],
        out_specs=pl.BlockSpec((tm, d), row),
        out_shape=jax.ShapeDtypeStruct((t, d), F32),
        compiler_params=pltpu.CompilerParams(dimension_semantics=("parallel",),
                                             vmem_limit_bytes=VMEM_LIMIT),
        name="post",
    )(x, oa, ob, oc, gmix, wo, gmlp, wup, wdn, gfin)


def _rope_tables(seq):
    inv_freq = ROPE_THETA ** (-jnp.arange(0, MLA_ROPE, 2, dtype=F32) / MLA_ROPE)
    ang = jnp.arange(seq, dtype=F32)[:, None] * inv_freq[None, :]
    cos, sin = jnp.cos(ang), jnp.sin(ang)
    half = MLA_ROPE // 2
    one = jnp.ones((seq, MLA_NOPE), F32)
    zero = jnp.zeros((seq, MLA_NOPE), F32)
    zh = jnp.zeros((seq, half), F32)
    tail1 = jnp.ones((seq, LANES - MLA_NOPE - MLA_ROPE), F32)
    tail0 = jnp.zeros((seq, LANES - MLA_NOPE - MLA_ROPE), F32)
    cos_t = jnp.concatenate([one, cos, cos, tail1], axis=1)
    sin_a = jnp.concatenate([zero, -sin, zh, tail0], axis=1)
    sin_b = jnp.concatenate([zero, zh, sin, tail0], axis=1)
    return cos_t, sin_a, sin_b


def _dil_bias(bias_tab):
    span = DIL_SPAN
    diff = span + jnp.arange(span)[:, None] - jnp.arange(2 * span)[None, :]
    in_band = (diff >= 0) & (diff <= span)
    tabs = []
    for _, dil in DILATED_PATTERNS:
        b = bias_tab[_bucket(diff * dil)].transpose(2, 0, 1).astype(F32)
        tabs.append(jnp.where(in_band[None], b, NEG))
    return jnp.stack(tabs)


def _moba_bias(bias_tab, nblk):
    blk = MOBA_BLOCK
    nfar = min(nblk - 1, REL_MAX_DISTANCE // blk + 1)
    delta = jnp.arange(nfar + 1)[:, None, None] * blk
    dist = delta + jnp.arange(blk)[None, :, None] - jnp.arange(blk)[None, None, :]
    return bias_tab[_bucket(dist)].transpose(3, 0, 1, 2).astype(F32)


def _pad_w_in(w):
    d = w.shape[0]
    sizes = (D_DIL, D_DIL, D_DIL, MLA_Q_LORA, MLA_KV_LORA, MLA_ROPE, D_MOBA, D_MOBA, D_MOBA)
    parts, lo = [], 0
    for n in sizes:
        parts.append(w[:, lo:lo + n])
        lo += n
    kr = jnp.concatenate([jnp.zeros((d, MLA_NOPE), w.dtype), parts[5],
                          jnp.zeros((d, LANES - MLA_NOPE - MLA_ROPE), w.dtype)], axis=1)
    parts[5] = kr
    return jnp.concatenate(parts, axis=1).astype(BF16)


def _pad_w_uq(w):
    r = w.shape[0]
    w = w.reshape(r, H_MLA, MLA_NOPE + MLA_ROPE)
    w = jnp.pad(w, ((0, 0), (0, 0), (0, LANES - MLA_NOPE - MLA_ROPE)))
    return w.reshape(r, H_MLA * LANES).astype(BF16)


def _split_w_ukv(w):
    r = w.shape[0]
    w = w.reshape(r, H_MLA, MLA_NOPE + MLA_V)
    wk = jnp.pad(w[:, :, :MLA_NOPE], ((0, 0), (0, 0), (0, LANES - MLA_NOPE))).reshape(r, H_MLA * LANES)
    wv = w[:, :, MLA_NOPE:].reshape(r, H_MLA * MLA_V)
    return wk.astype(BF16), wv.astype(BF16)


def kernel(x, g_attn, w_in, g_q_lora, g_kv_lora, w_uq, w_ukv, rel_bias, g_mix, w_o, g_mlp, w_up, w_down,
           g_final):
    batch, seq, d = x.shape
    depth = w_in.shape[0]
    assert seq % (DIL_SPAN * max(dil for _, dil in DILATED_PATTERNS)) == 0 and seq % 512 == 0
    cos_t, sin_a, sin_b = _rope_tables(seq)
    bias_dil = _dil_bias(rel_bias[:, :H_DIL])
    bias_moba = _moba_bias(rel_bias[:, H_DIL:], seq // MOBA_BLOCK)
    row = lambda v: v.reshape(1, -1).astype(F32)
    xf = x.reshape(batch * seq, d)
    for l in range(depth):
        wk, wv = _split_w_ukv(w_ukv[l])
        qa, ka, va, qm, km, vm, qc, kc, vc, kmean = _in_proj(
            xf, row(g_attn[l]), _pad_w_in(w_in[l]), row(g_q_lora[l]), row(g_kv_lora[l]),
            _pad_w_uq(w_uq[l]), wk, wv, cos_t, sin_a, sin_b, seq)
        oa = _dilated(qa, ka, va, bias_dil, batch, seq)
        ob = _mla(qm, km, vm, batch, seq)
        oc = _moba(qc, kc, vc, kmean, bias_moba, batch, seq)
        xf = _post(xf, oa, ob, oc, row(g_mix[l]), w_o[l].astype(BF16), row(g_mlp[l]),
                   w_up[l].astype(BF16), w_down[l].astype(BF16), row(g_final), final=(l == depth - 1))
    return xf.reshape(batch, seq, d)
```

```python
import functools
import math

import jax
import jax.numpy as jnp
from jax import lax
from jax.experimental import pallas as pl
from jax.experimental.pallas import tpu as pltpu

LANES = 128
HEAD_DIM = 64
H_DIL, H_MLA, H_MOBA = 6, 6, 4
D_DIL = H_DIL * HEAD_DIM
MLA_NOPE, MLA_ROPE, MLA_V = 64, 32, 64
MLA_Q_LORA, MLA_KV_LORA = 384, 128
D_MLA = H_MLA * MLA_V
D_MOBA = H_MOBA * HEAD_DIM
DILATED_PATTERNS = ((128, 1), (512, 4), (2048, 16))
DIL_SPAN = 128
MOBA_BLOCK = 256
MOBA_TOPK = 3
NUM_BUCKETS = 32
MAX_EXACT = 16
REL_MAX_DISTANCE = 2048
ROPE_THETA = 10000.0
EPS = 1e-6
NEG = -1e30
VMEM_LIMIT = 56 * 1024 * 1024

C_QA, C_KA, C_VA = 0, 384, 768
C_CQ, C_CKV, C_KR = 1152, 1536, 1664
C_QC, C_KC, C_VC = 1792, 2048, 2304
D_IN_PAD = 2560

BF16 = jnp.bfloat16
F32 = jnp.float32


def _bucket(dist):
    n = jnp.maximum(dist, 0)
    nf = jnp.maximum(n, 1).astype(F32)
    large = MAX_EXACT + (jnp.log(nf / MAX_EXACT) / math.log(REL_MAX_DISTANCE / MAX_EXACT)
                         * (NUM_BUCKETS - MAX_EXACT)).astype(jnp.int32)
    large = jnp.minimum(large, NUM_BUCKETS - 1)
    return jnp.where(n < MAX_EXACT, n, large)


def _rms(x, g):
    return x * lax.rsqrt(jnp.mean(x * x, axis=-1, keepdims=True) + EPS) * g


def _dot(a, b):
    return jnp.dot(a, b, preferred_element_type=F32)


def _dot_nt(a, b, precision=None):
    return lax.dot_general(a, b, (((1,), (1,)), ((), ())), preferred_element_type=F32,
                           precision=precision)


def _rope(x, cos, sin_a, sin_b):
    return x * cos + pltpu.roll(x, LANES - 16, 1) * sin_a + pltpu.roll(x, 16, 1) * sin_b


def _in_kernel(x_ref, g_ref, w_ref, gq_ref, gkv_ref, wuq_ref, wk_ref, wv_ref, cos_ref, sa_ref, sb_ref,
               qa_ref, ka_ref, va_ref, qm_ref, km_ref, vm_ref, qc_ref, kc_ref, vc_ref, kmean_ref):
    hb = _rms(x_ref[...], g_ref[...]).astype(BF16)

    def proj(c0, n):
        return _dot(hb, w_ref[:, c0:c0 + n])

    qa_ref[...] = proj(C_QA, D_DIL) * (HEAD_DIM ** -0.5)
    ka_ref[...] = proj(C_KA, D_DIL)
    va_ref[...] = proj(C_VA, D_DIL)

    cos, sin_a, sin_b = cos_ref[...], sa_ref[...], sb_ref[...]
    cq = _rms(proj(C_CQ, MLA_Q_LORA), gq_ref[...]).astype(BF16)
    q = _dot(cq, wuq_ref[...])
    ckv = _rms(proj(C_CKV, MLA_KV_LORA), gkv_ref[...]).astype(BF16)
    kn = _dot(ckv, wk_ref[...])
    vm_ref[...] = _dot(ckv, wv_ref[...]).astype(BF16)
    kr = _rope(proj(C_KR, LANES), cos, sin_a, sin_b)
    scale = (MLA_NOPE + MLA_ROPE) ** -0.5
    for h in range(H_MLA):
        sl = slice(h * LANES, (h + 1) * LANES)
        qm_ref[:, sl] = (_rope(q[:, sl], cos, sin_a, sin_b) * scale).astype(BF16)
        km_ref[:, sl] = (kn[:, sl] + kr).astype(BF16)

    qc_ref[...] = proj(C_QC, D_MOBA) * (HEAD_DIM ** -0.5)
    kc = proj(C_KC, D_MOBA)
    kc_ref[...] = kc.astype(BF16)
    vc_ref[...] = proj(C_VC, D_MOBA).astype(BF16)
    for i in range(kc.shape[0] // MOBA_BLOCK):
        kmean_ref[i] = jnp.mean(kc[i * MOBA_BLOCK:(i + 1) * MOBA_BLOCK], axis=0, keepdims=True)


def _in_proj(x, g, w, gq, gkv, wuq, wk, wv, cos, sin_a, sin_b, seq, tm=512):
    t, d = x.shape
    nt = seq // tm
    row = lambda i: (i, 0)
    const = lambda i: (0, 0)
    pos = lambda i: (i % nt, 0)
    full = lambda a: pl.BlockSpec(a.shape, const)
    outs = [(D_DIL, F32)] * 3 + [(H_MLA * LANES, BF16)] * 2 + [(D_MLA, BF16)] \
        + [(D_MOBA, F32), (D_MOBA, BF16), (D_MOBA, BF16)]
    out_shape = [jax.ShapeDtypeStruct((t, n), dt) for n, dt in outs]
    out_specs = [pl.BlockSpec((tm, n), row) for n, _ in outs]
    out_shape.append(jax.ShapeDtypeStruct((t // MOBA_BLOCK, 1, D_MOBA), F32))
    out_specs.append(pl.BlockSpec((tm // MOBA_BLOCK, 1, D_MOBA), lambda i: (i, 0, 0)))
    return pl.pallas_call(
        _in_kernel,
        grid=(t // tm,),
        in_specs=[pl.BlockSpec((tm, d), row), full(g), full(w), full(gq), full(gkv), full(wuq), full(wk),
                  full(wv), pl.BlockSpec((tm, LANES), pos), pl.BlockSpec((tm, LANES), pos),
                  pl.BlockSpec((tm, LANES), pos)],
        out_specs=out_specs,
        out_shape=out_shape,
        compiler_params=pltpu.CompilerParams(dimension_semantics=("parallel",),
                                             vmem_limit_bytes=VMEM_LIMIT),
        name="in_proj",
    )(x, g, w, gq, gkv, wuq, wk, wv, cos, sin_a, sin_b)


def _dil_kernel(q_ref, k_ref, v_ref, bias_ref, o_ref, oscr, lscr, *, unroll):
    seq = q_ref.shape[0]
    span = DIL_SPAN
    lane = lax.broadcasted_iota(jnp.int32, (span, LANES), 1)
    head0 = lane < HEAD_DIM

    for pi, (_, dil) in enumerate(DILATED_PATTERNS):
        nblk = seq // (span * dil)

        def rows(start, dil=dil):
            return pl.ds(start, span) if dil == 1 else pl.ds(start, span, stride=dil)

        def unit(u, carry, first, pi=pi, dil=dil, rows=rows):
            r = u if first else u % dil
            cur = r if first else (u // dil) * (span * dil) + r
            q = q_ref[rows(cur), :]
            qs = jnp.concatenate([jnp.where(head0, q, 0.0), jnp.where(head0, 0.0, q)], axis=0).astype(BF16)
            if first:
                kk = k_ref[rows(cur), :].astype(BF16)
                vv = v_ref[rows(cur), :].astype(BF16)
                s = _dot_nt(qs, kk) + bias_ref[pi, :, span:]
            else:
                prev = cur - span * dil
                kk = jnp.concatenate([k_ref[rows(prev), :], k_ref[rows(cur), :]], axis=0).astype(BF16)
                vv = jnp.concatenate([v_ref[rows(prev), :], v_ref[rows(cur), :]], axis=0).astype(BF16)
                s = _dot_nt(qs, kk) + bias_ref[pi]
            m = jnp.max(s, axis=-1, keepdims=True)
            p = jnp.exp(s - m)
            l = jnp.sum(p, axis=-1, keepdims=True)
            o = _dot(p.astype(BF16), vv) / l
            lse = m + jnp.log(l)
            oscr[pi, rows(cur), :] = jnp.where(head0, o[:span], o[span:])
            lscr[pi, rows(cur), :] = jnp.where(head0, lse[:span], lse[span:])
            return carry

        lax.fori_loop(0, dil, functools.partial(unit, first=True), 0, unroll=min(unroll, dil))
        lax.fori_loop(dil, nblk * dil, functools.partial(unit, first=False), 0, unroll=unroll)

    ct = 512
    for c in range(seq // ct):
        sl = pl.ds(c * ct, ct)
        l0, l1, l2 = lscr[0, sl, :], lscr[1, sl, :], lscr[2, sl, :]
        mx = jnp.maximum(jnp.maximum(l0, l1), l2)
        e0, e1, e2 = jnp.exp(l0 - mx), jnp.exp(l1 - mx), jnp.exp(l2 - mx)
        num = e0 * oscr[0, sl, :] + e1 * oscr[1, sl, :] + e2 * oscr[2, sl, :]
        o_ref[sl, :] = num / (e0 + e1 + e2)


def _dilated(qa, ka, va, bias, batch, seq, unroll=2):
    t = qa.shape[0]
    npair = H_DIL // 2
    blk = pl.BlockSpec((seq, LANES), lambda b, p: (b, p))
    return pl.pallas_call(
        functools.partial(_dil_kernel, unroll=unroll),
        grid=(batch, npair),
        in_specs=[blk, blk, blk,
                  pl.BlockSpec((None, len(DILATED_PATTERNS), 2 * DIL_SPAN, 2 * DIL_SPAN),
                               lambda b, p: (p, 0, 0, 0))],
        out_specs=blk,
        out_shape=jax.ShapeDtypeStruct((t, D_DIL), F32),
        scratch_shapes=[pltpu.VMEM((len(DILATED_PATTERNS), seq, LANES), F32),
                        pltpu.VMEM((len(DILATED_PATTERNS), seq, LANES), F32)],
        compiler_params=pltpu.CompilerParams(dimension_semantics=("parallel", "parallel"),
                                             vmem_limit_bytes=VMEM_LIMIT),
        name="dilated",
    )(qa, ka, va, bias)


def _mla_kernel(q_ref, k_ref, v_ref, o_ref, *, tq, tk):
    i = pl.program_id(2)
    lane = lax.broadcasted_iota(jnp.int32, (tq, LANES), 1)
    rowi = lax.broadcasted_iota(jnp.int32, (tq, tk), 0)
    coli = lax.broadcasted_iota(jnp.int32, (tq, tk), 1)
    qs = [q_ref[:, hh * LANES:(hh + 1) * LANES] for hh in range(2)]
    nfull = (i * tq) // tk
    causal = coli <= rowi + (i * tq - nfull * tk)

    def step(j, carry, masked):
        ks = pl.ds(pl.multiple_of(j * tk, tk), tk)
        v = v_ref[ks, :]
        new = []
        for hh in range(2):
            m, l, acc = carry[hh]
            s = _dot_nt(qs[hh], k_ref[ks, hh * LANES:(hh + 1) * LANES])
            if masked:
                s = jnp.where(causal, s, NEG)
            m_new = jnp.maximum(m, jnp.max(s, axis=-1, keepdims=True))
            a = jnp.exp(m - m_new)
            p = jnp.exp(s - m_new)
            l = a * l + jnp.sum(p, axis=-1, keepdims=True)
            acc = a * acc + _dot(p.astype(BF16), v)
            new.append((m_new, l, acc))
        return tuple(new)

    init = (jnp.full((tq, 1), NEG, F32), jnp.zeros((tq, 1), F32), jnp.zeros((tq, LANES), F32))
    carry = lax.fori_loop(0, nfull, functools.partial(step, masked=False), (init, init))
    (_, l0, acc0), (_, l1, acc1) = step(nfull, carry, masked=True)
    o_ref[...] = jnp.where(lane < MLA_V, acc0 / l0, acc1 / l1)


def _mla(qm, km, vm, batch, seq, tq=512, tk=512):
    t = qm.shape[0]
    nq = seq // tq
    npair = H_MLA // 2
    assert tk % tq == 0 and seq % tk == 0
    return pl.pallas_call(
        functools.partial(_mla_kernel, tq=tq, tk=tk),
        grid=(batch, npair, nq),
        in_specs=[pl.BlockSpec((tq, 2 * LANES), lambda b, p, i: (b * nq + i, p)),
                  pl.BlockSpec((seq, 2 * LANES), lambda b, p, i: (b, p)),
                  pl.BlockSpec((seq, LANES), lambda b, p, i: (b, p))],
        out_specs=pl.BlockSpec((tq, LANES), lambda b, p, i: (b * nq + i, p)),
        out_shape=jax.ShapeDtypeStruct((t, D_MLA), F32),
        compiler_params=pltpu.CompilerParams(dimension_semantics=("parallel", "parallel", "arbitrary"),
                                             vmem_limit_bytes=VMEM_LIMIT),
        name="mla",
    )(qm, km, vm)


def _moba_kernel(q_ref, k_ref, v_ref, kmean_ref, bias_ref, o_ref, *, nfar):
    blk = MOBA_BLOCK
    j = pl.program_id(2)
    nblk = kmean_ref.shape[0]
    q = q_ref[...]
    lane = lax.broadcasted_iota(jnp.int32, (blk, LANES), 1)
    head0 = lane < HEAD_DIM
    qs = jnp.concatenate([jnp.where(head0, q, 0.0), jnp.where(head0, 0.0, q)], axis=0)

    km = jnp.concatenate([kmean_ref[:, 0, :], jnp.zeros((LANES - nblk, LANES), F32)], axis=0)
    bidx = lax.broadcasted_iota(jnp.int32, (2 * blk, LANES), 1)
    gate = _dot_nt(qs, km, precision=lax.Precision.HIGHEST)
    gate = jnp.where(bidx < j, gate, -jnp.inf)
    unsel = jnp.ones((2 * blk, LANES), F32)
    for _ in range(MOBA_TOPK):
        mx = jnp.max(gate, axis=-1, keepdims=True)
        first = jnp.min(jnp.where(gate == mx, bidx, LANES), axis=-1, keepdims=True)
        hit = bidx == first
        unsel = jnp.where(hit & (mx > -jnp.inf), 0.0, unsel)
        gate = jnp.where(hit, -jnp.inf, gate)

    lhs = jnp.concatenate([qs.astype(BF16), unsel.astype(BF16)], axis=1)
    klane = lax.broadcasted_iota(jnp.int32, (blk, LANES), 1)

    def scores(n, pick):
        ks = pl.ds(pl.multiple_of(n * blk, blk), blk)
        rhs = jnp.concatenate([k_ref[ks, :], jnp.where(klane == pick, NEG, 0.0).astype(BF16)], axis=1)
        return _dot_nt(lhs, rhs), v_ref[ks, :]

    s, v = scores(j, -1)
    s = s + bias_ref[0]
    m = jnp.max(s, axis=-1, keepdims=True)
    p = jnp.exp(s - m)
    l = jnp.sum(p, axis=-1, keepdims=True)
    acc = _dot(p.astype(BF16), v)

    def step(n, carry):
        m, l, acc = carry
        s, v = scores(n, n)
        s = s + bias_ref[jnp.minimum(j - n, nfar)]
        m_new = jnp.maximum(m, jnp.max(s, axis=-1, keepdims=True))
        a = jnp.exp(m - m_new)
        p = jnp.exp(s - m_new)
        l = a * l + jnp.sum(p, axis=-1, keepdims=True)
        acc = a * acc + _dot(p.astype(BF16), v)
        return m_new, l, acc

    m, l, acc = lax.fori_loop(0, j, step, (m, l, acc))
    o = acc / l
    o_ref[...] = jnp.where(head0, o[:blk], o[blk:])


def _moba(qc, kc, vc, kmean, bias, batch, seq):
    t = qc.shape[0]
    nblk = seq // MOBA_BLOCK
    npair = H_MOBA // 2
    nfar = bias.shape[1] - 1
    return pl.pallas_call(
        functools.partial(_moba_kernel, nfar=nfar),
        grid=(npair, batch, nblk),
        in_specs=[pl.BlockSpec((MOBA_BLOCK, LANES), lambda p, b, j: (b * nblk + j, p)),
                  pl.BlockSpec((seq, LANES), lambda p, b, j: (b, p)),
                  pl.BlockSpec((seq, LANES), lambda p, b, j: (b, p)),
                  pl.BlockSpec((nblk, 1, LANES), lambda p, b, j: (b, 0, p)),
                  pl.BlockSpec((None, nfar + 1, 2 * MOBA_BLOCK, MOBA_BLOCK), lambda p, b, j: (p, 0, 0, 0))],
        out_specs=pl.BlockSpec((MOBA_BLOCK, LANES), lambda p, b, j: (b * nblk + j, p)),
        out_shape=jax.ShapeDtypeStruct((t, D_MOBA), F32),
        compiler_params=pltpu.CompilerParams(dimension_semantics=("parallel", "parallel", "arbitrary"),
                                             vmem_limit_bytes=VMEM_LIMIT),
        name="moba",
    )(qc, kc, vc, kmean, bias)


def _post_kernel(x_ref, oa_ref, ob_ref, oc_ref, gmix_ref, wo_ref, gmlp_ref, wup_ref, wdn_ref, gfin_ref,
                 y_ref, *, final, ff_chunk):
    gm = gmix_ref[...]
    x = x_ref[...]
    lo = 0
    for o_ref in (oa_ref, ob_ref, oc_ref):
        n = o_ref.shape[1]
        mixed = _rms(o_ref[...], gm[:, lo:lo + n]).astype(BF16)
        x = x + _dot(mixed, wo_ref[lo:lo + n, :])
        lo += n
    hb = _rms(x, gmlp_ref[...]).astype(BF16)
    y_ref[...] = x
    for c in range(wup_ref.shape[1] // ff_chunk):
        cs = slice(c * ff_chunk, (c + 1) * ff_chunk)
        u = jnp.maximum(_dot(hb, wup_ref[:, cs]), 0.0)
        y_ref[...] += _dot((u * u).astype(BF16), wdn_ref[cs, :])
    if final:
        y_ref[...] = _rms(y_ref[...], gfin_ref[...])


def _post(x, oa, ob, oc, gmix, wo, gmlp, wup, wdn, gfin, final, tm=512, ff_chunk=1024):
    t, d = x.shape
    row = lambda i: (i, 0)
    const = lambda i: (0, 0)
    full = lambda a: pl.BlockSpec(a.shape, const)
    weight = lambda a: pl.BlockSpec(a.shape, const, pipeline_mode=pl.Buffered(1))
    return pl.pallas_call(
        functools.partial(_post_kernel, final=final, ff_chunk=ff_chunk),
        grid=(t // tm,),
        in_specs=[pl.BlockSpec((tm, d), row), pl.BlockSpec((tm, oa.shape[1]), row),
                  pl.BlockSpec((tm, ob.shape[1]), row), pl.BlockSpec((tm, oc.shape[1]), row),
                  full(gmix), weight(wo), full(gmlp), weight(wup), weight(wdn), full(gfin)],
        out_specs=pl.BlockSpec((tm, d), row),
        out_shape=jax.ShapeDtypeStruct((t, d), F32),
        compiler_params=pltpu.CompilerParams(dimension_semantics=("parallel",),
                                             vmem_limit_bytes=VMEM_LIMIT),
        name="post",
    )(x, oa, ob, oc, gmix, wo, gmlp, wup, wdn, gfin)


def _rope_tables(seq):
    inv_freq = ROPE_THETA ** (-jnp.arange(0, MLA_ROPE, 2, dtype=F32) / MLA_ROPE)
    ang = jnp.arange(seq, dtype=F32)[:, None] * inv_freq[None, :]
    cos, sin = jnp.cos(ang), jnp.sin(ang)
    half = MLA_ROPE // 2
    one = jnp.ones((seq, MLA_NOPE), F32)
    zero = jnp.zeros((seq, MLA_NOPE), F32)
    zh = jnp.zeros((seq, half), F32)
    tail1 = jnp.ones((seq, LANES - MLA_NOPE - MLA_ROPE), F32)
    tail0 = jnp.zeros((seq, LANES - MLA_NOPE - MLA_ROPE), F32)
    cos_t = jnp.concatenate([one, cos, cos, tail1], axis=1)
    sin_a = jnp.concatenate([zero, -sin, zh, tail0], axis=1)
    sin_b = jnp.concatenate([zero, zh, sin, tail0], axis=1)
    return cos_t, sin_a, sin_b


def _lookup(tab, bucket):
    col = lambda b: tab[b][(slice(None),) + (None,) * bucket.ndim]
    out = jnp.broadcast_to(col(0), (tab.shape[1],) + bucket.shape)
    for b in range(1, NUM_BUCKETS):
        out = jnp.where(bucket[None] == b, col(b), out)
    return out.astype(F32)


def _dil_bias(bias_tab):
    span = DIL_SPAN
    npat = len(DILATED_PATTERNS)
    diff = span + jnp.arange(span)[:, None] - jnp.arange(2 * span)[None, :]
    in_band = (diff >= 0) & (diff <= span)
    tabs = []
    for _, dil in DILATED_PATTERNS:
        tabs.append(jnp.where(in_band[None], _lookup(bias_tab, _bucket(diff * dil)), NEG))
    b = jnp.stack(tabs, axis=1)
    b = b.reshape(H_DIL // 2, 2, npat, span, 2 * span).transpose(0, 2, 1, 3, 4)
    return b.reshape(H_DIL // 2, npat, 2 * span, 2 * span)


def _moba_bias(bias_tab, nblk):
    blk = MOBA_BLOCK
    nfar = min(nblk - 1, REL_MAX_DISTANCE // blk + 1)
    delta = jnp.arange(nfar + 1)[:, None, None] * blk
    dist = delta + jnp.arange(blk)[None, :, None] - jnp.arange(blk)[None, None, :]
    b = jnp.where(dist[None] >= 0, _lookup(bias_tab, _bucket(dist)), NEG)
    b = b.reshape(H_MOBA // 2, 2, nfar + 1, blk, blk).transpose(0, 2, 1, 3, 4)
    return b.reshape(H_MOBA // 2, nfar + 1, 2 * blk, blk)


def _pad_w_in(w):
    d = w.shape[0]
    sizes = (D_DIL, D_DIL, D_DIL, MLA_Q_LORA, MLA_KV_LORA, MLA_ROPE, D_MOBA, D_MOBA, D_MOBA)
    parts, lo = [], 0
    for n in sizes:
        parts.append(w[:, lo:lo + n])
        lo += n
    kr = jnp.concatenate([jnp.zeros((d, MLA_NOPE), w.dtype), parts[5],
                          jnp.zeros((d, LANES - MLA_NOPE - MLA_ROPE), w.dtype)], axis=1)
    parts[5] = kr
    return jnp.concatenate(parts, axis=1).astype(BF16)


def _pad_w_uq(w):
    r = w.shape[0]
    w = w.reshape(r, H_MLA, MLA_NOPE + MLA_ROPE)
    w = jnp.pad(w, ((0, 0), (0, 0), (0, LANES - MLA_NOPE - MLA_ROPE)))
    return w.reshape(r, H_MLA * LANES).astype(BF16)


def _split_w_ukv(w):
    r = w.shape[0]
    w = w.reshape(r, H_MLA, MLA_NOPE + MLA_V)
    wk = jnp.pad(w[:, :, :MLA_NOPE], ((0, 0), (0, 0), (0, LANES - MLA_NOPE))).reshape(r, H_MLA * LANES)
    wv = w[:, :, MLA_NOPE:].reshape(r, H_MLA * MLA_V)
    return wk.astype(BF16), wv.astype(BF16)


def kernel(x, g_attn, w_in, g_q_lora, g_kv_lora, w_uq, w_ukv, rel_bias, g_mix, w_o, g_mlp, w_up, w_down,
           g_final):
    batch, seq, d = x.shape
    depth = w_in.shape[0]
    assert seq % (DIL_SPAN * max(dil for _, dil in DILATED_PATTERNS)) == 0 and seq % 512 == 0
    cos_t, sin_a, sin_b = _rope_tables(seq)
    bias_dil = _dil_bias(rel_bias[:, :H_DIL])
    bias_moba = _moba_bias(rel_bias[:, H_DIL:], seq // MOBA_BLOCK)
    row = lambda v: v.reshape(1, -1).astype(F32)
    xf = x.reshape(batch * seq, d)
    for l in range(depth):
        wk, wv = _split_w_ukv(w_ukv[l])
        qa, ka, va, qm, km, vm, qc, kc, vc, kmean = _in_proj(
            xf, row(g_attn[l]), _pad_w_in(w_in[l]), row(g_q_lora[l]), row(g_kv_lora[l]),
            _pad_w_uq(w_uq[l]), wk, wv, cos_t, sin_a, sin_b, seq)
        oa = _dilated(qa, ka, va, bias_dil, batch, seq)
        ob = _mla(qm, km, vm, batch, seq)
        oc = _moba(qc, kc, vc, kmean, bias_moba, batch, seq)
        xf = _post(xf, oa, ob, oc, row(g_mix[l]), w_o[l].astype(BF16), row(g_mlp[l]),
                   w_up[l].astype(BF16), w_down[l].astype(BF16), row(g_final), final=(l == depth - 1))
    return xf.reshape(batch, seq, d)
```

```python
import functools
import math

import jax
import jax.numpy as jnp
from jax import lax
from jax.experimental import pallas as pl
from jax.experimental.pallas import tpu as pltpu

LANES = 128
HEAD_DIM = 64
H_DIL, H_MLA, H_MOBA = 6, 6, 4
D_DIL = H_DIL * HEAD_DIM
MLA_NOPE, MLA_ROPE, MLA_V = 64, 32, 64
MLA_Q_LORA, MLA_KV_LORA = 384, 128
D_MLA = H_MLA * MLA_V
D_MOBA = H_MOBA * HEAD_DIM
DILATED_PATTERNS = ((128, 1), (512, 4), (2048, 16))
DIL_SPAN = 128
MOBA_BLOCK = 256
MOBA_TOPK = 3
NUM_BUCKETS = 32
MAX_EXACT = 16
REL_MAX_DISTANCE = 2048
ROPE_THETA = 10000.0
EPS = 1e-6
NEG = -1e30
LOG2E = math.log2(math.e)
VMEM_LIMIT = 56 * 1024 * 1024
TOKEN_TILE = 512
MLA_TQ = 512

C_QA, C_KA, C_VA = 0, 384, 768
C_CQ, C_CKV, C_KR = 1152, 1536, 1664
C_QC, C_KC, C_VC = 1792, 2048, 2304
D_IN_PAD = 2560

BF16 = jnp.bfloat16
F32 = jnp.float32


def _bucket(dist):
    n = jnp.maximum(dist, 0)
    nf = jnp.maximum(n, 1).astype(F32)
    large = MAX_EXACT + (jnp.log(nf / MAX_EXACT) / math.log(REL_MAX_DISTANCE / MAX_EXACT)
                         * (NUM_BUCKETS - MAX_EXACT)).astype(jnp.int32)
    large = jnp.minimum(large, NUM_BUCKETS - 1)
    return jnp.where(n < MAX_EXACT, n, large)


def _rms(x, g):
    return x * lax.rsqrt(jnp.mean(x * x, axis=-1, keepdims=True) + EPS) * g


def _dot(a, b):
    return jnp.dot(a, b, preferred_element_type=F32)


def _dot_nt(a, b, precision=None):
    return lax.dot_general(a, b, (((1,), (1,)), ((), ())), preferred_element_type=F32,
                           precision=precision)


def _rope(x, cos, sin_a, sin_b):
    return x * cos + pltpu.roll(x, LANES - 16, 1) * sin_a + pltpu.roll(x, 16, 1) * sin_b


def _in_kernel(x_ref, g_ref, w_ref, gq_ref, gkv_ref, wuqT_ref, wk_ref, wvT_ref, wqcT_ref, wvcT_ref,
               cos_ref, sa_ref, sb_ref, cosT_ref, sinT_ref,
               qa_ref, ka_ref, va_ref, qmT_ref, km_ref, vmT_ref, qcT_ref, kc_ref, vcT_ref, kmean_ref):
    hb = _rms(x_ref[...], g_ref[...]).astype(BF16)

    def proj(c0, n):
        return _dot(hb, w_ref[:, c0:c0 + n])

    qa_ref[...] = proj(C_QA, D_DIL) * (HEAD_DIM ** -0.5 * LOG2E)
    ka_ref[...] = proj(C_KA, D_DIL)
    va_ref[...] = proj(C_VA, D_DIL)

    cq = _rms(proj(C_CQ, MLA_Q_LORA), gq_ref[...]).astype(BF16)
    qT = _dot_nt(wuqT_ref[...], cq)
    cosT, sinT = cosT_ref[...], sinT_ref[...]
    scale = (MLA_NOPE + MLA_ROPE) ** -0.5 * LOG2E
    half = MLA_ROPE // 2
    for h in range(H_MLA):
        r0 = h * LANES
        x1 = qT[r0 + MLA_NOPE:r0 + MLA_NOPE + half]
        x2 = qT[r0 + MLA_NOPE + half:r0 + MLA_NOPE + MLA_ROPE]
        roped = jnp.concatenate([qT[r0:r0 + MLA_NOPE], x1 * cosT - x2 * sinT, x2 * cosT + x1 * sinT,
                                 qT[r0 + MLA_NOPE + MLA_ROPE:r0 + LANES]], axis=0)
        qmT_ref[r0:r0 + LANES, :] = (roped * scale).astype(BF16)
    ckv = _rms(proj(C_CKV, MLA_KV_LORA), gkv_ref[...]).astype(BF16)
    kn = _dot(ckv, wk_ref[...])
    vmT_ref[0] = _dot_nt(wvT_ref[...], ckv).astype(BF16)
    kr = _rope(proj(C_KR, LANES), cos_ref[...], sa_ref[...], sb_ref[...])
    for h in range(H_MLA):
        sl = slice(h * LANES, (h + 1) * LANES)
        km_ref[:, sl] = (kn[:, sl] + kr).astype(BF16)

    qcT_ref[...] = _dot_nt(wqcT_ref[...], hb) * (HEAD_DIM ** -0.5 * LOG2E)
    kc = proj(C_KC, D_MOBA)
    kc_ref[...] = kc.astype(BF16)
    vcT = _dot_nt(wvcT_ref[...], hb).astype(BF16)
    for i in range(kc.shape[0] // MOBA_BLOCK):
        sl = slice(i * MOBA_BLOCK, (i + 1) * MOBA_BLOCK)
        vcT_ref[i] = vcT[:, sl]
        kmean_ref[i] = jnp.mean(kc[sl], axis=0, keepdims=True)


def _in_proj(x, g, w, gq, gkv, wuqT, wk, wvT, wqcT, wvcT, cos, sin_a, sin_b, cosT, sinT, seq, tm):
    t, d = x.shape
    nt = seq // tm
    row = lambda i: (i, 0)
    col = lambda i: (0, i)
    lead = lambda i: (i, 0, 0)
    const = lambda i: (0, 0)
    pos = lambda i: (i % nt, 0)
    posT = lambda i: (0, i % nt)
    full = lambda a: pl.BlockSpec(a.shape, const)
    nb = tm // MOBA_BLOCK
    outs = [
        ((t, D_DIL), F32, (tm, D_DIL), row), ((t, D_DIL), F32, (tm, D_DIL), row),
        ((t, D_DIL), F32, (tm, D_DIL), row),
        ((H_MLA * LANES, t), BF16, (H_MLA * LANES, tm), col),
        ((t, H_MLA * LANES), BF16, (tm, H_MLA * LANES), row),
        ((t // tm, D_MLA, tm), BF16, (1, D_MLA, tm), lead),
        ((D_MOBA, t), F32, (D_MOBA, tm), col),
        ((t, D_MOBA), BF16, (tm, D_MOBA), row),
        ((t // MOBA_BLOCK, D_MOBA, MOBA_BLOCK), BF16, (nb, D_MOBA, MOBA_BLOCK), lead),
        ((t // MOBA_BLOCK, 1, D_MOBA), F32, (nb, 1, D_MOBA), lead),
    ]
    return pl.pallas_call(
        _in_kernel,
        grid=(t // tm,),
        in_specs=[pl.BlockSpec((tm, d), row), full(g), full(w), full(gq), full(gkv), full(wuqT), full(wk),
                  full(wvT), full(wqcT), full(wvcT),
                  pl.BlockSpec((tm, LANES), pos), pl.BlockSpec((tm, LANES), pos), pl.BlockSpec((tm, LANES), pos),
                  pl.BlockSpec((MLA_ROPE // 2, tm), posT), pl.BlockSpec((MLA_ROPE // 2, tm), posT)],
        out_specs=[pl.BlockSpec(blk, imap) for _, _, blk, imap in outs],
        out_shape=[jax.ShapeDtypeStruct(shape, dt) for shape, dt, _, _ in outs],
        compiler_params=pltpu.CompilerParams(dimension_semantics=("parallel",),
                                             vmem_limit_bytes=VMEM_LIMIT),
        name="in_proj",
    )(x, g, w, gq, gkv, wuqT, wk, wvT, wqcT, wvcT, cos, sin_a, sin_b, cosT, sinT)


def _dil_kernel(q_ref, k_ref, v_ref, bias_ref, o_ref, oscr, lscr, *, unroll):
    seq = q_ref.shape[0]
    span = DIL_SPAN
    lane = lax.broadcasted_iota(jnp.int32, (span, LANES), 1)
    head0 = lane < HEAD_DIM

    for pi, (_, dil) in enumerate(DILATED_PATTERNS):
        nblk = seq // (span * dil)

        def rows(start, dil=dil):
            return pl.ds(start, span) if dil == 1 else pl.ds(start, span, stride=dil)

        def unit(u, carry, first, pi=pi, dil=dil, rows=rows):
            r = u if first else u % dil
            cur = r if first else (u // dil) * (span * dil) + r
            q = q_ref[rows(cur), :]
            qs = jnp.concatenate([jnp.where(head0, q, 0.0), jnp.where(head0, 0.0, q)], axis=0).astype(BF16)
            if first:
                kk = k_ref[rows(cur), :].astype(BF16)
                vv = v_ref[rows(cur), :].astype(BF16)
                s = _dot_nt(qs, kk) + bias_ref[pi, :, span:]
            else:
                prev = cur - span * dil
                kk = jnp.concatenate([k_ref[rows(prev), :], k_ref[rows(cur), :]], axis=0).astype(BF16)
                vv = jnp.concatenate([v_ref[rows(prev), :], v_ref[rows(cur), :]], axis=0).astype(BF16)
                s = _dot_nt(qs, kk) + bias_ref[pi]
            m = jnp.max(s, axis=-1, keepdims=True)
            p = jnp.exp2(s - m)
            l = jnp.sum(p, axis=-1, keepdims=True)
            o = _dot(p.astype(BF16), vv) / l
            lse = m + jnp.log(l) * LOG2E
            oscr[pi, rows(cur), :] = jnp.where(head0, o[:span], o[span:])
            lscr[pi, rows(cur), :] = jnp.where(head0, lse[:span], lse[span:])
            return carry

        lax.fori_loop(0, dil, functools.partial(unit, first=True), 0, unroll=min(unroll, dil))
        lax.fori_loop(dil, nblk * dil, functools.partial(unit, first=False), 0, unroll=unroll)

    ct = 512
    for c in range(seq // ct):
        sl = pl.ds(c * ct, ct)
        l0, l1, l2 = lscr[0, sl, :], lscr[1, sl, :], lscr[2, sl, :]
        mx = jnp.maximum(jnp.maximum(l0, l1), l2)
        e0, e1, e2 = jnp.exp2(l0 - mx), jnp.exp2(l1 - mx), jnp.exp2(l2 - mx)
        num = e0 * oscr[0, sl, :] + e1 * oscr[1, sl, :] + e2 * oscr[2, sl, :]
        o_ref[sl, :] = num / (e0 + e1 + e2)


def _dilated(qa, ka, va, bias, batch, seq, unroll=2):
    t = qa.shape[0]
    npair = H_DIL // 2
    blk = pl.BlockSpec((seq, LANES), lambda b, p: (b, p))
    return pl.pallas_call(
        functools.partial(_dil_kernel, unroll=unroll),
        grid=(batch, npair),
        in_specs=[blk, blk, blk,
                  pl.BlockSpec((None, len(DILATED_PATTERNS), 2 * DIL_SPAN, 2 * DIL_SPAN),
                               lambda b, p: (p, 0, 0, 0))],
        out_specs=blk,
        out_shape=jax.ShapeDtypeStruct((t, D_DIL), F32),
        scratch_shapes=[pltpu.VMEM((len(DILATED_PATTERNS), seq, LANES), F32),
                        pltpu.VMEM((len(DILATED_PATTERNS), seq, LANES), F32)],
        compiler_params=pltpu.CompilerParams(dimension_semantics=("parallel", "parallel"),
                                             vmem_limit_bytes=VMEM_LIMIT),
        name="dilated",
    )(qa, ka, va, bias)


def _mla_kernel(qT_ref, k_ref, vT_ref, o_ref, *, tq, tk, nsub):
    i = pl.program_id(2)
    sub = tk // nsub
    krow = lax.broadcasted_iota(jnp.int32, (tk, 2 * tq), 0)
    qcol = lax.broadcasted_iota(jnp.int32, (tk, 2 * tq), 1)
    qcol = jnp.where(qcol >= tq, qcol - tq, qcol)
    nfull = (i * tq) // tk
    causal = krow <= qcol + (i * tq - nfull * tk)
    zero = jnp.zeros((LANES, tq), BF16)
    rhs = jnp.concatenate([jnp.concatenate([qT_ref[:LANES, :], zero], axis=1),
                           jnp.concatenate([zero, qT_ref[LANES:, :]], axis=1)], axis=0)

    def step(j, carry, masked):
        m, l, acc = carry
        vT = vT_ref[j]
        ss = []
        for u in range(nsub):
            s = _dot(k_ref[pl.ds(pl.multiple_of(j * tk + u * sub, sub), sub), :], rhs)
            if masked:
                s = jnp.where(causal[u * sub:(u + 1) * sub], s, NEG)
            ss.append(s)
        for u, s in enumerate(ss):
            m_new = jnp.maximum(m, jnp.max(s, axis=0, keepdims=True))
            a = jnp.exp2(m - m_new)
            p = jnp.exp2(s - m_new)
            l = a * l + jnp.sum(p, axis=0, keepdims=True)
            p = p.astype(BF16)
            vu = vT[:, u * sub:(u + 1) * sub]
            pv = jnp.concatenate([_dot(vu[:MLA_V], p[:, :tq]), _dot(vu[MLA_V:], p[:, tq:])], axis=1)
            acc = a * acc + pv
            m = m_new
        return m, l, acc

    init = (jnp.full((1, 2 * tq), NEG, F32), jnp.zeros((1, 2 * tq), F32), jnp.zeros((MLA_V, 2 * tq), F32))
    carry = lax.fori_loop(0, nfull, functools.partial(step, masked=False), init)
    _, l, acc = step(nfull, carry, masked=True)
    oT = acc / l
    o_ref[...] = jnp.concatenate([oT[:, :tq], oT[:, tq:]], axis=0).T


def _mla(qmT, km, vmT, batch, seq, tq, tk, nsub=4):
    t = km.shape[0]
    nq = seq // tq
    npair = H_MLA // 2
    assert tk % tq == 0 and seq % tk == 0
    return pl.pallas_call(
        functools.partial(_mla_kernel, tq=tq, tk=tk, nsub=nsub),
        grid=(batch, npair, nq),
        in_specs=[pl.BlockSpec((2 * LANES, tq), lambda b, p, i: (p, b * nq + i)),
                  pl.BlockSpec((seq, 2 * LANES), lambda b, p, i: (b, p)),
                  pl.BlockSpec((seq // tk, LANES, tk), lambda b, p, i: (b, p, 0))],
        out_specs=pl.BlockSpec((tq, LANES), lambda b, p, i: (b * nq + i, p)),
        out_shape=jax.ShapeDtypeStruct((t, D_MLA), F32),
        compiler_params=pltpu.CompilerParams(dimension_semantics=("parallel", "parallel", "arbitrary"),
                                             vmem_limit_bytes=VMEM_LIMIT),
        name="mla",
    )(qmT, km, vmT)


def _moba_kernel(qT_ref, k_ref, vT_ref, kmean_ref, bias_ref, o_ref, *, nfar):
    blk = MOBA_BLOCK
    j = pl.program_id(2)
    nblk = kmean_ref.shape[0]
    qT = qT_ref[...]
    head0 = lax.broadcasted_iota(jnp.int32, (LANES, blk), 0) < HEAD_DIM
    qsT = jnp.concatenate([jnp.where(head0, qT, 0.0), jnp.where(head0, 0.0, qT)], axis=1)

    gate = jnp.dot(kmean_ref[:, 0, :], qsT, preferred_element_type=F32, precision=lax.Precision.HIGHEST)
    bidx = lax.broadcasted_iota(jnp.int32, (nblk, 2 * blk), 0)
    gate = jnp.where(bidx < j, gate, -jnp.inf)
    unsel = jnp.ones((nblk, 2 * blk), F32)
    for _ in range(MOBA_TOPK):
        mx = jnp.max(gate, axis=0, keepdims=True)
        first = jnp.min(jnp.where(gate == mx, bidx, nblk), axis=0, keepdims=True)
        hit = bidx == first
        unsel = jnp.where(hit & (mx > -jnp.inf), 0.0, unsel)
        gate = jnp.where(hit, -jnp.inf, gate)

    rhs = jnp.concatenate([qsT.astype(BF16), unsel.astype(BF16),
                           jnp.ones((LANES - nblk, 2 * blk), BF16)], axis=0)
    klane = lax.broadcasted_iota(jnp.int32, (blk, LANES), 1)

    def scores(n, pick):
        lhs = jnp.concatenate([k_ref[pl.ds(pl.multiple_of(n * blk, blk), blk), :],
                               jnp.where(klane == pick, NEG, 0.0).astype(BF16)], axis=1)
        return _dot(lhs, rhs) + bias_ref[jnp.minimum(j - n, nfar)]

    def weighted(n, p):
        vT = vT_ref[n]
        return jnp.concatenate([_dot(vT[:HEAD_DIM], p[:, :blk]), _dot(vT[HEAD_DIM:], p[:, blk:])], axis=1)

    def pair(t, carry):
        m, l, acc = carry
        n0 = j - 2 * t
        n1 = jnp.maximum(n0 - 1, 0)
        s0 = scores(n0, jnp.where(t == 0, -1, n0))
        s1 = scores(n1, jnp.where(n0 == 0, nblk, n1))
        for n, s in ((n0, s0), (n1, s1)):
            m_new = jnp.maximum(m, jnp.max(s, axis=0, keepdims=True))
            a = jnp.exp2(m - m_new)
            p = jnp.exp2(s - m_new)
            l = a * l + jnp.sum(p, axis=0, keepdims=True)
            acc = a * acc + weighted(n, p.astype(BF16))
            m = m_new
        return m, l, acc

    init = (jnp.full((1, 2 * blk), NEG, F32), jnp.zeros((1, 2 * blk), F32), jnp.zeros((HEAD_DIM, 2 * blk), F32))
    _, l, acc = lax.fori_loop(0, j // 2 + 1, pair, init)
    oT = acc / l
    o_ref[...] = jnp.concatenate([oT[:, :blk], oT[:, blk:]], axis=0).T


def _moba(qcT, kc, vcT, kmean, bias, batch, seq):
    t = kc.shape[0]
    nblk = seq // MOBA_BLOCK
    npair = H_MOBA // 2
    nfar = bias.shape[1] - 1
    return pl.pallas_call(
        functools.partial(_moba_kernel, nfar=nfar),
        grid=(npair, batch, nblk),
        in_specs=[pl.BlockSpec((LANES, MOBA_BLOCK), lambda p, b, j: (p, b * nblk + j)),
                  pl.BlockSpec((seq, LANES), lambda p, b, j: (b, p)),
                  pl.BlockSpec((nblk, LANES, MOBA_BLOCK), lambda p, b, j: (b, p, 0)),
                  pl.BlockSpec((nblk, 1, LANES), lambda p, b, j: (b, 0, p)),
                  pl.BlockSpec((None, nfar + 1, MOBA_BLOCK, 2 * MOBA_BLOCK), lambda p, b, j: (p, 0, 0, 0))],
        out_specs=pl.BlockSpec((MOBA_BLOCK, LANES), lambda p, b, j: (b * nblk + j, p)),
        out_shape=jax.ShapeDtypeStruct((t, D_MOBA), F32),
        compiler_params=pltpu.CompilerParams(dimension_semantics=("parallel", "parallel", "arbitrary"),
                                             vmem_limit_bytes=VMEM_LIMIT),
        name="moba",
    )(qcT, kc, vcT, kmean, bias)


def _post_kernel(x_ref, oa_ref, ob_ref, oc_ref, gmix_ref, wo_ref, gmlp_ref, wup_ref, wdn_ref, gfin_ref,
                 y_ref, *, final, ff_chunk):
    gm = gmix_ref[...]
    x = x_ref[...]
    lo = 0
    for o_ref in (oa_ref, ob_ref, oc_ref):
        n = o_ref.shape[1]
        mixed = _rms(o_ref[...], gm[:, lo:lo + n]).astype(BF16)
        x = x + _dot(mixed, wo_ref[lo:lo + n, :])
        lo += n
    hb = _rms(x, gmlp_ref[...]).astype(BF16)
    y_ref[...] = x
    for c in range(wup_ref.shape[1] // ff_chunk):
        cs = slice(c * ff_chunk, (c + 1) * ff_chunk)
        u = jnp.maximum(_dot(hb, wup_ref[:, cs]), 0.0)
        y_ref[...] += _dot((u * u).astype(BF16), wdn_ref[cs, :])
    if final:
        y_ref[...] = _rms(y_ref[...], gfin_ref[...])


def _post(x, oa, ob, oc, gmix, wo, gmlp, wup, wdn, gfin, final, tm=512, ff_chunk=1024):
    t, d = x.shape
    row = lambda i: (i, 0)
    const = lambda i: (0, 0)
    full = lambda a: pl.BlockSpec(a.shape, const)
    weight = lambda a: pl.BlockSpec(a.shape, const, pipeline_mode=pl.Buffered(1))
    return pl.pallas_call(
        functools.partial(_post_kernel, final=final, ff_chunk=ff_chunk),
        grid=(t // tm,),
        in_specs=[pl.BlockSpec((tm, d), row), pl.BlockSpec((tm, oa.shape[1]), row),
                  pl.BlockSpec((tm, ob.shape[1]), row), pl.BlockSpec((tm, oc.shape[1]), row),
                  full(gmix), weight(wo), full(gmlp), weight(wup), weight(wdn), full(gfin)],
        out_specs=pl.BlockSpec((tm, d), row),
        out_shape=jax.ShapeDtypeStruct((t, d), F32),
        compiler_params=pltpu.CompilerParams(dimension_semantics=("parallel",),
                                             vmem_limit_bytes=VMEM_LIMIT),
        name="post",
    )(x, oa, ob, oc, gmix, wo, gmlp, wup, wdn, gfin)


def _rope_tables(seq):
    inv_freq = ROPE_THETA ** (-jnp.arange(0, MLA_ROPE, 2, dtype=F32) / MLA_ROPE)
    ang = jnp.arange(seq, dtype=F32)[:, None] * inv_freq[None, :]
    cos, sin = jnp.cos(ang), jnp.sin(ang)
    half = MLA_ROPE // 2
    one = jnp.ones((seq, MLA_NOPE), F32)
    zero = jnp.zeros((seq, MLA_NOPE), F32)
    zh = jnp.zeros((seq, half), F32)
    tail1 = jnp.ones((seq, LANES - MLA_NOPE - MLA_ROPE), F32)
    tail0 = jnp.zeros((seq, LANES - MLA_NOPE - MLA_ROPE), F32)
    cos_t = jnp.concatenate([one, cos, cos, tail1], axis=1)
    sin_a = jnp.concatenate([zero, -sin, zh, tail0], axis=1)
    sin_b = jnp.concatenate([zero, zh, sin, tail0], axis=1)
    return cos_t, sin_a, sin_b, cos.T, sin.T


def _lookup(tab, bucket):
    col = lambda b: tab[b][(slice(None),) + (None,) * bucket.ndim]
    out = jnp.broadcast_to(col(0), (tab.shape[1],) + bucket.shape)
    for b in range(1, NUM_BUCKETS):
        out = jnp.where(bucket[None] == b, col(b), out)
    return out.astype(F32)


def _dil_bias(bias_tab):
    span = DIL_SPAN
    npat = len(DILATED_PATTERNS)
    diff = span + jnp.arange(span)[:, None] - jnp.arange(2 * span)[None, :]
    in_band = (diff >= 0) & (diff <= span)
    tabs = []
    for _, dil in DILATED_PATTERNS:
        tabs.append(jnp.where(in_band[None], _lookup(bias_tab, _bucket(diff * dil)) * LOG2E, NEG))
    b = jnp.stack(tabs, axis=1)
    b = b.reshape(H_DIL // 2, 2, npat, span, 2 * span).transpose(0, 2, 1, 3, 4)
    return b.reshape(H_DIL // 2, npat, 2 * span, 2 * span)


def _moba_bias(bias_tab, nblk):
    blk = MOBA_BLOCK
    nfar = min(nblk - 1, REL_MAX_DISTANCE // blk + 1)
    delta = jnp.arange(nfar + 1)[:, None, None] * blk
    dist = delta + jnp.arange(blk)[None, :, None] - jnp.arange(blk)[None, None, :]
    b = jnp.where(dist[None] >= 0, _lookup(bias_tab, _bucket(dist)) * LOG2E, NEG)
    b = b.reshape(H_MOBA // 2, 2, nfar + 1, blk, blk).transpose(0, 2, 4, 1, 3)
    return b.reshape(H_MOBA // 2, nfar + 1, blk, 2 * blk)


def _pad_w_in(w):
    d = w.shape[0]
    sizes = (D_DIL, D_DIL, D_DIL, MLA_Q_LORA, MLA_KV_LORA, MLA_ROPE, D_MOBA, D_MOBA, D_MOBA)
    parts, lo = [], 0
    for n in sizes:
        parts.append(w[:, lo:lo + n])
        lo += n
    kr = jnp.concatenate([jnp.zeros((d, MLA_NOPE), w.dtype), parts[5],
                          jnp.zeros((d, LANES - MLA_NOPE - MLA_ROPE), w.dtype)], axis=1)
    parts[5] = kr
    return jnp.concatenate(parts, axis=1).astype(BF16)


def _pad_w_uq(w):
    r = w.shape[0]
    w = w.reshape(r, H_MLA, MLA_NOPE + MLA_ROPE)
    w = jnp.pad(w, ((0, 0), (0, 0), (0, LANES - MLA_NOPE - MLA_ROPE)))
    return w.reshape(r, H_MLA * LANES).astype(BF16)


def _split_w_ukv(w):
    r = w.shape[0]
    w = w.reshape(r, H_MLA, MLA_NOPE + MLA_V)
    wk = jnp.pad(w[:, :, :MLA_NOPE], ((0, 0), (0, 0), (0, LANES - MLA_NOPE))).reshape(r, H_MLA * LANES)
    wv = w[:, :, MLA_NOPE:].reshape(r, H_MLA * MLA_V)
    return wk.astype(BF16), wv.T.astype(BF16)


def kernel(x, g_attn, w_in, g_q_lora, g_kv_lora, w_uq, w_ukv, rel_bias, g_mix, w_o, g_mlp, w_up, w_down,
           g_final):
    batch, seq, d = x.shape
    depth = w_in.shape[0]
    assert seq % (DIL_SPAN * max(dil for _, dil in DILATED_PATTERNS)) == 0 and seq % TOKEN_TILE == 0
    cos_t, sin_a, sin_b, cosT, sinT = _rope_tables(seq)
    bias_dil = _dil_bias(rel_bias[:, :H_DIL])
    bias_moba = _moba_bias(rel_bias[:, H_DIL:], seq // MOBA_BLOCK)
    row = lambda v: v.reshape(1, -1).astype(F32)
    xf = x.reshape(batch * seq, d)
    for l in range(depth):
        w = _pad_w_in(w_in[l])
        wk, wvT = _split_w_ukv(w_ukv[l])
        qa, ka, va, qmT, km, vmT, qcT, kc, vcT, kmean = _in_proj(
            xf, row(g_attn[l]), w, row(g_q_lora[l]), row(g_kv_lora[l]), _pad_w_uq(w_uq[l]).T, wk, wvT,
            w[:, C_QC:C_QC + D_MOBA].T, w[:, C_VC:C_VC + D_MOBA].T, cos_t, sin_a, sin_b, cosT, sinT, seq,
            tm=TOKEN_TILE)
        oa = _dilated(qa, ka, va, bias_dil, batch, seq)
        ob = _mla(qmT, km, vmT, batch, seq, tq=MLA_TQ, tk=TOKEN_TILE)
        oc = _moba(qcT, kc, vcT, kmean, bias_moba, batch, seq)
        xf = _post(xf, oa, ob, oc, row(g_mix[l]), w_o[l].astype(BF16), row(g_mlp[l]),
                   w_up[l].astype(BF16), w_down[l].astype(BF16), row(g_final), final=(l == depth - 1))
    return xf.reshape(batch, seq, d)
```

```python
import functools
import math

import jax
import jax.numpy as jnp
from jax import lax
from jax.experimental import pallas as pl
from jax.experimental.pallas import tpu as pltpu

LANES = 128
HEAD_DIM = 64
H_DIL, H_MLA, H_MOBA = 6, 6, 4
D_DIL = H_DIL * HEAD_DIM
MLA_NOPE, MLA_ROPE, MLA_V = 64, 32, 64
MLA_Q_LORA, MLA_KV_LORA = 384, 128
D_MLA = H_MLA * MLA_V
D_MOBA = H_MOBA * HEAD_DIM
DILATED_PATTERNS = ((128, 1), (512, 4), (2048, 16))
DIL_SPAN = 128
MOBA_BLOCK = 256
MOBA_TOPK = 3
NUM_BUCKETS = 32
MAX_EXACT = 16
REL_MAX_DISTANCE = 2048
ROPE_THETA = 10000.0
EPS = 1e-6
NEG = -1e30
LOG2E = math.log2(math.e)
ONES_ROWS = 16
VMEM_LIMIT = 56 * 1024 * 1024
TOKEN_TILE = 512
MLA_TQ = 512

C_QA, C_KA, C_VA = 0, 384, 768
C_CQ, C_CKV, C_KR = 1152, 1536, 1664
C_QC, C_KC, C_VC = 1792, 2048, 2304
D_IN_PAD = 2560

BF16 = jnp.bfloat16
F32 = jnp.float32


def _bucket(dist):
    n = jnp.maximum(dist, 0)
    nf = jnp.maximum(n, 1).astype(F32)
    large = MAX_EXACT + (jnp.log(nf / MAX_EXACT) / math.log(REL_MAX_DISTANCE / MAX_EXACT)
                         * (NUM_BUCKETS - MAX_EXACT)).astype(jnp.int32)
    large = jnp.minimum(large, NUM_BUCKETS - 1)
    return jnp.where(n < MAX_EXACT, n, large)


def _rms(x, g):
    return x * lax.rsqrt(jnp.mean(x * x, axis=-1, keepdims=True) + EPS) * g


def _dot(a, b):
    return jnp.dot(a, b, preferred_element_type=F32)


def _dot_nt(a, b, precision=None):
    return lax.dot_general(a, b, (((1,), (1,)), ((), ())), preferred_element_type=F32,
                           precision=precision)


def _rope(x, cos, sin_a, sin_b):
    return x * cos + pltpu.roll(x, LANES - 16, 1) * sin_a + pltpu.roll(x, 16, 1) * sin_b


def _in_kernel(x_ref, g_ref, w_ref, gq_ref, gkv_ref, wuqT_ref, wk_ref, wvT_ref, wqcT_ref, wvcT_ref,
               cos_ref, sa_ref, sb_ref, cosT_ref, sinT_ref,
               qa_ref, ka_ref, va_ref, qmT_ref, km_ref, vmT_ref, qcT_ref, kc_ref, vcT_ref, kmean_ref):
    hb = _rms(x_ref[...], g_ref[...]).astype(BF16)

    def proj(c0, n):
        return _dot(hb, w_ref[:, c0:c0 + n])

    qa_ref[...] = proj(C_QA, D_DIL) * (HEAD_DIM ** -0.5 * LOG2E)
    ka_ref[...] = proj(C_KA, D_DIL)
    va_ref[...] = proj(C_VA, D_DIL)

    cq = _rms(proj(C_CQ, MLA_Q_LORA), gq_ref[...]).astype(BF16)
    qT = _dot_nt(wuqT_ref[...], cq)
    cosT, sinT = cosT_ref[...], sinT_ref[...]
    scale = (MLA_NOPE + MLA_ROPE) ** -0.5 * LOG2E
    half = MLA_ROPE // 2
    for h in range(H_MLA):
        r0 = h * LANES
        x1 = qT[r0 + MLA_NOPE:r0 + MLA_NOPE + half]
        x2 = qT[r0 + MLA_NOPE + half:r0 + MLA_NOPE + MLA_ROPE]
        roped = jnp.concatenate([qT[r0:r0 + MLA_NOPE], x1 * cosT - x2 * sinT, x2 * cosT + x1 * sinT,
                                 qT[r0 + MLA_NOPE + MLA_ROPE:r0 + LANES]], axis=0)
        qmT_ref[r0:r0 + LANES, :] = (roped * scale).astype(BF16)
    ckv = _rms(proj(C_CKV, MLA_KV_LORA), gkv_ref[...]).astype(BF16)
    kn = _dot(ckv, wk_ref[...])
    vmT_ref[0] = _dot_nt(wvT_ref[...], ckv).astype(BF16)
    kr = _rope(proj(C_KR, LANES), cos_ref[...], sa_ref[...], sb_ref[...])
    for h in range(H_MLA):
        sl = slice(h * LANES, (h + 1) * LANES)
        km_ref[:, sl] = (kn[:, sl] + kr).astype(BF16)

    qcT_ref[...] = _dot_nt(wqcT_ref[...], hb) * (HEAD_DIM ** -0.5 * LOG2E)
    kc = proj(C_KC, D_MOBA)
    kc_ref[...] = kc.astype(BF16)
    vcT = _dot_nt(wvcT_ref[...], hb).astype(BF16)
    for i in range(kc.shape[0] // MOBA_BLOCK):
        sl = slice(i * MOBA_BLOCK, (i + 1) * MOBA_BLOCK)
        vcT_ref[i] = vcT[:, sl]
        kmean_ref[i] = jnp.mean(kc[sl], axis=0, keepdims=True)


def _in_proj(x, g, w, gq, gkv, wuqT, wk, wvT, wqcT, wvcT, cos, sin_a, sin_b, cosT, sinT, seq, tm):
    t, d = x.shape
    nt = seq // tm
    row = lambda i: (i, 0)
    col = lambda i: (0, i)
    lead = lambda i: (i, 0, 0)
    const = lambda i: (0, 0)
    pos = lambda i: (i % nt, 0)
    posT = lambda i: (0, i % nt)
    full = lambda a: pl.BlockSpec(a.shape, const)
    nb = tm // MOBA_BLOCK
    outs = [
        ((t, D_DIL), F32, (tm, D_DIL), row), ((t, D_DIL), F32, (tm, D_DIL), row),
        ((t, D_DIL), F32, (tm, D_DIL), row),
        ((H_MLA * LANES, t), BF16, (H_MLA * LANES, tm), col),
        ((t, H_MLA * LANES), BF16, (tm, H_MLA * LANES), row),
        ((t // tm, D_MLA, tm), BF16, (1, D_MLA, tm), lead),
        ((D_MOBA, t), F32, (D_MOBA, tm), col),
        ((t, D_MOBA), BF16, (tm, D_MOBA), row),
        ((t // MOBA_BLOCK, D_MOBA, MOBA_BLOCK), BF16, (nb, D_MOBA, MOBA_BLOCK), lead),
        ((t // MOBA_BLOCK, 1, D_MOBA), F32, (nb, 1, D_MOBA), lead),
    ]
    return pl.pallas_call(
        _in_kernel,
        grid=(t // tm,),
        in_specs=[pl.BlockSpec((tm, d), row), full(g), full(w), full(gq), full(gkv), full(wuqT), full(wk),
                  full(wvT), full(wqcT), full(wvcT),
                  pl.BlockSpec((tm, LANES), pos), pl.BlockSpec((tm, LANES), pos), pl.BlockSpec((tm, LANES), pos),
                  pl.BlockSpec((MLA_ROPE // 2, tm), posT), pl.BlockSpec((MLA_ROPE // 2, tm), posT)],
        out_specs=[pl.BlockSpec(blk, imap) for _, _, blk, imap in outs],
        out_shape=[jax.ShapeDtypeStruct(shape, dt) for shape, dt, _, _ in outs],
        compiler_params=pltpu.CompilerParams(dimension_semantics=("parallel",),
                                             vmem_limit_bytes=VMEM_LIMIT),
        name="in_proj",
    )(x, g, w, gq, gkv, wuqT, wk, wvT, wqcT, wvcT, cos, sin_a, sin_b, cosT, sinT)


def _dil_kernel(q_ref, k_ref, v_ref, bias_ref, o_ref, oscr, lscr, *, unroll):
    seq = q_ref.shape[0]
    span = DIL_SPAN
    lane = lax.broadcasted_iota(jnp.int32, (span, LANES), 1)
    head0 = lane < HEAD_DIM

    for pi, (_, dil) in enumerate(DILATED_PATTERNS):
        nblk = seq // (span * dil)

        def rows(start, dil=dil):
            return pl.ds(start, span) if dil == 1 else pl.ds(start, span, stride=dil)

        def unit(u, carry, first, pi=pi, dil=dil, rows=rows):
            r = u if first else u % dil
            cur = r if first else (u // dil) * (span * dil) + r
            q = q_ref[rows(cur), :]
            qs = jnp.concatenate([jnp.where(head0, q, 0.0), jnp.where(head0, 0.0, q)], axis=0).astype(BF16)
            if first:
                kk = k_ref[rows(cur), :].astype(BF16)
                vv = v_ref[rows(cur), :].astype(BF16)
                s = _dot_nt(qs, kk) + bias_ref[pi, :, span:]
            else:
                prev = cur - span * dil
                kk = jnp.concatenate([k_ref[rows(prev), :], k_ref[rows(cur), :]], axis=0).astype(BF16)
                vv = jnp.concatenate([v_ref[rows(prev), :], v_ref[rows(cur), :]], axis=0).astype(BF16)
                s = _dot_nt(qs, kk) + bias_ref[pi]
            m = jnp.max(s, axis=-1, keepdims=True)
            p = jnp.exp2(s - m).astype(BF16)
            on = _dot(p, jnp.concatenate([vv, jnp.ones_like(vv)], axis=1))
            l = on[:, LANES:]
            o = on[:, :LANES] / l
            lse = m + jnp.log(l) * LOG2E
            oscr[pi, rows(cur), :] = jnp.where(head0, o[:span], o[span:])
            lscr[pi, rows(cur), :] = jnp.where(head0, lse[:span], lse[span:])
            return carry

        lax.fori_loop(0, dil, functools.partial(unit, first=True), 0, unroll=min(unroll, dil))
        lax.fori_loop(dil, nblk * dil, functools.partial(unit, first=False), 0, unroll=unroll)

    ct = 512
    for c in range(seq // ct):
        sl = pl.ds(c * ct, ct)
        l0, l1, l2 = lscr[0, sl, :], lscr[1, sl, :], lscr[2, sl, :]
        mx = jnp.maximum(jnp.maximum(l0, l1), l2)
        e0, e1, e2 = jnp.exp2(l0 - mx), jnp.exp2(l1 - mx), jnp.exp2(l2 - mx)
        num = e0 * oscr[0, sl, :] + e1 * oscr[1, sl, :] + e2 * oscr[2, sl, :]
        o_ref[sl, :] = num / (e0 + e1 + e2)


def _dilated(qa, ka, va, bias, batch, seq, unroll=2):
    t = qa.shape[0]
    npair = H_DIL // 2
    blk = pl.BlockSpec((seq, LANES), lambda b, p: (b, p))
    return pl.pallas_call(
        functools.partial(_dil_kernel, unroll=unroll),
        grid=(batch, npair),
        in_specs=[blk, blk, blk,
                  pl.BlockSpec((None, len(DILATED_PATTERNS), 2 * DIL_SPAN, 2 * DIL_SPAN),
                               lambda b, p: (p, 0, 0, 0))],
        out_specs=blk,
        out_shape=jax.ShapeDtypeStruct((t, D_DIL), F32),
        scratch_shapes=[pltpu.VMEM((len(DILATED_PATTERNS), seq, LANES), F32),
                        pltpu.VMEM((len(DILATED_PATTERNS), seq, LANES), F32)],
        compiler_params=pltpu.CompilerParams(dimension_semantics=("parallel", "parallel"),
                                             vmem_limit_bytes=VMEM_LIMIT),
        name="dilated",
    )(qa, ka, va, bias)


def _mla_kernel(qT_ref, k_ref, vT_ref, o_ref, *, tq, tk, nsub):
    i = pl.program_id(2)
    sub = tk // nsub
    krow = lax.broadcasted_iota(jnp.int32, (tk, 2 * tq), 0)
    qcol = lax.broadcasted_iota(jnp.int32, (tk, 2 * tq), 1)
    qcol = jnp.where(qcol >= tq, qcol - tq, qcol)
    nfull = (i * tq) // tk
    causal = krow <= qcol + (i * tq - nfull * tk)
    zero = jnp.zeros((LANES, tq), BF16)
    rhs = jnp.concatenate([jnp.concatenate([qT_ref[:LANES, :], zero], axis=1),
                           jnp.concatenate([zero, qT_ref[LANES:, :]], axis=1)], axis=0)

    ones = jnp.ones((ONES_ROWS, sub), BF16)

    def step(j, carry, masked):
        m, acc = carry
        vT = vT_ref[j]
        ss = []
        for u in range(nsub):
            s = _dot(k_ref[pl.ds(pl.multiple_of(j * tk + u * sub, sub), sub), :], rhs)
            if masked:
                s = jnp.where(causal[u * sub:(u + 1) * sub], s, NEG)
            ss.append(s)
        for u, s in enumerate(ss):
            m_new = jnp.maximum(m, jnp.max(s, axis=0, keepdims=True))
            p = jnp.exp2(s - m_new).astype(BF16)
            vu = vT[:, u * sub:(u + 1) * sub]
            pv = jnp.concatenate([_dot(jnp.concatenate([vu[:MLA_V], ones], axis=0), p[:, :tq]),
                                  _dot(jnp.concatenate([vu[MLA_V:], ones], axis=0), p[:, tq:])], axis=1)
            acc = jnp.exp2(m - m_new) * acc + pv
            m = m_new
        return m, acc

    init = (jnp.full((1, 2 * tq), NEG, F32), jnp.zeros((MLA_V + ONES_ROWS, 2 * tq), F32))
    carry = lax.fori_loop(0, nfull, functools.partial(step, masked=False), init)
    _, acc = step(nfull, carry, masked=True)
    oT = acc[:MLA_V] / acc[MLA_V:MLA_V + 1]
    o_ref[...] = jnp.concatenate([oT[:, :tq], oT[:, tq:]], axis=0).T


def _mla(qmT, km, vmT, batch, seq, tq, tk, nsub=2):
    t = km.shape[0]
    nq = seq // tq
    npair = H_MLA // 2
    assert tk % tq == 0 and seq % tk == 0
    return pl.pallas_call(
        functools.partial(_mla_kernel, tq=tq, tk=tk, nsub=nsub),
        grid=(batch, npair, nq),
        in_specs=[pl.BlockSpec((2 * LANES, tq), lambda b, p, i: (p, b * nq + i)),
                  pl.BlockSpec((seq, 2 * LANES), lambda b, p, i: (b, p)),
                  pl.BlockSpec((seq // tk, LANES, tk), lambda b, p, i: (b, p, 0))],
        out_specs=pl.BlockSpec((tq, LANES), lambda b, p, i: (b * nq + i, p)),
        out_shape=jax.ShapeDtypeStruct((t, D_MLA), F32),
        compiler_params=pltpu.CompilerParams(dimension_semantics=("parallel", "parallel", "arbitrary"),
                                             vmem_limit_bytes=VMEM_LIMIT),
        name="mla",
    )(qmT, km, vmT)


def _moba_kernel(qT_ref, k_ref, vT_ref, kmean_ref, bias_ref, o_ref, *, nfar):
    blk = MOBA_BLOCK
    j = pl.program_id(2)
    nblk = kmean_ref.shape[0]
    qT = qT_ref[...]
    head0 = lax.broadcasted_iota(jnp.int32, (LANES, blk), 0) < HEAD_DIM
    qsT = jnp.concatenate([jnp.where(head0, qT, 0.0), jnp.where(head0, 0.0, qT)], axis=1)

    gate = jnp.dot(kmean_ref[:, 0, :], qsT, preferred_element_type=F32, precision=lax.Precision.HIGHEST)
    bidx = lax.broadcasted_iota(jnp.int32, (nblk, 2 * blk), 0)
    gate = jnp.where(bidx < j, gate, -jnp.inf)
    unsel = jnp.ones((nblk, 2 * blk), F32)
    for _ in range(MOBA_TOPK):
        mx = jnp.max(gate, axis=0, keepdims=True)
        first = jnp.min(jnp.where(gate == mx, bidx, nblk), axis=0, keepdims=True)
        hit = bidx == first
        unsel = jnp.where(hit & (mx > -jnp.inf), 0.0, unsel)
        gate = jnp.where(hit, -jnp.inf, gate)

    rhs = jnp.concatenate([qsT.astype(BF16), unsel.astype(BF16),
                           jnp.ones((LANES - nblk, 2 * blk), BF16)], axis=0)
    klane = lax.broadcasted_iota(jnp.int32, (blk, LANES), 1)

    def scores(n, pick):
        lhs = jnp.concatenate([k_ref[pl.ds(pl.multiple_of(n * blk, blk), blk), :],
                               jnp.where(klane == pick, NEG, 0.0).astype(BF16)], axis=1)
        return _dot(lhs, rhs) + bias_ref[jnp.minimum(j - n, nfar)]

    ones = jnp.ones((ONES_ROWS, blk), BF16)

    def weighted(n, p):
        vT = vT_ref[n]
        return jnp.concatenate([_dot(jnp.concatenate([vT[:HEAD_DIM], ones], axis=0), p[:, :blk]),
                                _dot(jnp.concatenate([vT[HEAD_DIM:], ones], axis=0), p[:, blk:])], axis=1)

    def pair(t, carry):
        m, acc = carry
        n0 = j - 2 * t
        n1 = jnp.maximum(n0 - 1, 0)
        s0 = scores(n0, jnp.where(t == 0, -1, n0))
        s1 = scores(n1, jnp.where(n0 == 0, nblk, n1))
        for n, s in ((n0, s0), (n1, s1)):
            m_new = jnp.maximum(m, jnp.max(s, axis=0, keepdims=True))
            p = jnp.exp2(s - m_new).astype(BF16)
            acc = jnp.exp2(m - m_new) * acc + weighted(n, p)
            m = m_new
        return m, acc

    init = (jnp.full((1, 2 * blk), NEG, F32), jnp.zeros((HEAD_DIM + ONES_ROWS, 2 * blk), F32))
    _, acc = lax.fori_loop(0, j // 2 + 1, pair, init)
    oT = acc[:HEAD_DIM] / acc[HEAD_DIM:HEAD_DIM + 1]
    o_ref[...] = jnp.concatenate([oT[:, :blk], oT[:, blk:]], axis=0).T


def _moba(qcT, kc, vcT, kmean, bias, batch, seq):
    t = kc.shape[0]
    nblk = seq // MOBA_BLOCK
    npair = H_MOBA // 2
    nfar = bias.shape[1] - 1
    return pl.pallas_call(
        functools.partial(_moba_kernel, nfar=nfar),
        grid=(npair, batch, nblk),
        in_specs=[pl.BlockSpec((LANES, MOBA_BLOCK), lambda p, b, j: (p, b * nblk + j)),
                  pl.BlockSpec((seq, LANES), lambda p, b, j: (b, p)),
                  pl.BlockSpec((nblk, LANES, MOBA_BLOCK), lambda p, b, j: (b, p, 0)),
                  pl.BlockSpec((nblk, 1, LANES), lambda p, b, j: (b, 0, p)),
                  pl.BlockSpec((None, nfar + 1, MOBA_BLOCK, 2 * MOBA_BLOCK), lambda p, b, j: (p, 0, 0, 0))],
        out_specs=pl.BlockSpec((MOBA_BLOCK, LANES), lambda p, b, j: (b * nblk + j, p)),
        out_shape=jax.ShapeDtypeStruct((t, D_MOBA), F32),
        compiler_params=pltpu.CompilerParams(dimension_semantics=("parallel", "parallel", "arbitrary"),
                                             vmem_limit_bytes=VMEM_LIMIT),
        name="moba",
    )(qcT, kc, vcT, kmean, bias)


def _post_kernel(x_ref, oa_ref, ob_ref, oc_ref, gmix_ref, wo_ref, gmlp_ref, wup_ref, wdn_ref, gfin_ref,
                 y_ref, *, final, ff_chunk):
    gm = gmix_ref[...]
    x = x_ref[...]
    lo = 0
    for o_ref in (oa_ref, ob_ref, oc_ref):
        n = o_ref.shape[1]
        mixed = _rms(o_ref[...], gm[:, lo:lo + n]).astype(BF16)
        x = x + _dot(mixed, wo_ref[lo:lo + n, :])
        lo += n
    hb = _rms(x, gmlp_ref[...]).astype(BF16)
    y_ref[...] = x
    for c in range(wup_ref.shape[1] // ff_chunk):
        cs = slice(c * ff_chunk, (c + 1) * ff_chunk)
        u = jnp.maximum(_dot(hb, wup_ref[:, cs]), 0.0)
        y_ref[...] += _dot((u * u).astype(BF16), wdn_ref[cs, :])
    if final:
        y_ref[...] = _rms(y_ref[...], gfin_ref[...])


def _post(x, oa, ob, oc, gmix, wo, gmlp, wup, wdn, gfin, final, tm=512, ff_chunk=1024):
    t, d = x.shape
    row = lambda i: (i, 0)
    const = lambda i: (0, 0)
    full = lambda a: pl.BlockSpec(a.shape, const)
    weight = lambda a: pl.BlockSpec(a.shape, const, pipeline_mode=pl.Buffered(1))
    return pl.pallas_call(
        functools.partial(_post_kernel, final=final, ff_chunk=ff_chunk),
        grid=(t // tm,),
        in_specs=[pl.BlockSpec((tm, d), row), pl.BlockSpec((tm, oa.shape[1]), row),
                  pl.BlockSpec((tm, ob.shape[1]), row), pl.BlockSpec((tm, oc.shape[1]), row),
                  full(gmix), weight(wo), full(gmlp), weight(wup), weight(wdn), full(gfin)],
        out_specs=pl.BlockSpec((tm, d), row),
        out_shape=jax.ShapeDtypeStruct((t, d), F32),
        compiler_params=pltpu.CompilerParams(dimension_semantics=("parallel",),
                                             vmem_limit_bytes=VMEM_LIMIT),
        name="post",
    )(x, oa, ob, oc, gmix, wo, gmlp, wup, wdn, gfin)


def _rope_tables(seq):
    inv_freq = ROPE_THETA ** (-jnp.arange(0, MLA_ROPE, 2, dtype=F32) / MLA_ROPE)
    ang = jnp.arange(seq, dtype=F32)[:, None] * inv_freq[None, :]
    cos, sin = jnp.cos(ang), jnp.sin(ang)
    half = MLA_ROPE // 2
    one = jnp.ones((seq, MLA_NOPE), F32)
    zero = jnp.zeros((seq, MLA_NOPE), F32)
    zh = jnp.zeros((seq, half), F32)
    tail1 = jnp.ones((seq, LANES - MLA_NOPE - MLA_ROPE), F32)
    tail0 = jnp.zeros((seq, LANES - MLA_NOPE - MLA_ROPE), F32)
    cos_t = jnp.concatenate([one, cos, cos, tail1], axis=1)
    sin_a = jnp.concatenate([zero, -sin, zh, tail0], axis=1)
    sin_b = jnp.concatenate([zero, zh, sin, tail0], axis=1)
    return cos_t, sin_a, sin_b, cos.T, sin.T


def _lookup(tab, bucket):
    col = lambda b: tab[b][(slice(None),) + (None,) * bucket.ndim]
    out = jnp.broadcast_to(col(0), (tab.shape[1],) + bucket.shape)
    for b in range(1, NUM_BUCKETS):
        out = jnp.where(bucket[None] == b, col(b), out)
    return out.astype(F32)


def _dil_bias(bias_tab):
    span = DIL_SPAN
    npat = len(DILATED_PATTERNS)
    diff = span + jnp.arange(span)[:, None] - jnp.arange(2 * span)[None, :]
    in_band = (diff >= 0) & (diff <= span)
    tabs = []
    for _, dil in DILATED_PATTERNS:
        tabs.append(jnp.where(in_band[None], _lookup(bias_tab, _bucket(diff * dil)) * LOG2E, NEG))
    b = jnp.stack(tabs, axis=1)
    b = b.reshape(H_DIL // 2, 2, npat, span, 2 * span).transpose(0, 2, 1, 3, 4)
    return b.reshape(H_DIL // 2, npat, 2 * span, 2 * span)


def _moba_bias(bias_tab, nblk):
    blk = MOBA_BLOCK
    nfar = min(nblk - 1, REL_MAX_DISTANCE // blk + 1)
    delta = jnp.arange(nfar + 1)[:, None, None] * blk
    dist = delta + jnp.arange(blk)[None, :, None] - jnp.arange(blk)[None, None, :]
    b = jnp.where(dist[None] >= 0, _lookup(bias_tab, _bucket(dist)) * LOG2E, NEG)
    b = b.reshape(H_MOBA // 2, 2, nfar + 1, blk, blk).transpose(0, 2, 4, 1, 3)
    return b.reshape(H_MOBA // 2, nfar + 1, blk, 2 * blk)


def _pad_w_in(w):
    d = w.shape[0]
    sizes = (D_DIL, D_DIL, D_DIL, MLA_Q_LORA, MLA_KV_LORA, MLA_ROPE, D_MOBA, D_MOBA, D_MOBA)
    parts, lo = [], 0
    for n in sizes:
        parts.append(w[:, lo:lo + n])
        lo += n
    kr = jnp.concatenate([jnp.zeros((d, MLA_NOPE), w.dtype), parts[5],
                          jnp.zeros((d, LANES - MLA_NOPE - MLA_ROPE), w.dtype)], axis=1)
    parts[5] = kr
    return jnp.concatenate(parts, axis=1).astype(BF16)


def _pad_w_uq(w):
    r = w.shape[0]
    w = w.reshape(r, H_MLA, MLA_NOPE + MLA_ROPE)
    w = jnp.pad(w, ((0, 0), (0, 0), (0, LANES - MLA_NOPE - MLA_ROPE)))
    return w.reshape(r, H_MLA * LANES).astype(BF16)


def _split_w_ukv(w):
    r = w.shape[0]
    w = w.reshape(r, H_MLA, MLA_NOPE + MLA_V)
    wk = jnp.pad(w[:, :, :MLA_NOPE], ((0, 0), (0, 0), (0, LANES - MLA_NOPE))).reshape(r, H_MLA * LANES)
    wv = w[:, :, MLA_NOPE:].reshape(r, H_MLA * MLA_V)
    return wk.astype(BF16), wv.T.astype(BF16)


def kernel(x, g_attn, w_in, g_q_lora, g_kv_lora, w_uq, w_ukv, rel_bias, g_mix, w_o, g_mlp, w_up, w_down,
           g_final):
    batch, seq, d = x.shape
    depth = w_in.shape[0]
    assert seq % (DIL_SPAN * max(dil for _, dil in DILATED_PATTERNS)) == 0 and seq % TOKEN_TILE == 0
    cos_t, sin_a, sin_b, cosT, sinT = _rope_tables(seq)
    bias_dil = _dil_bias(rel_bias[:, :H_DIL])
    bias_moba = _moba_bias(rel_bias[:, H_DIL:], seq // MOBA_BLOCK)
    row = lambda v: v.reshape(1, -1).astype(F32)
    xf = x.reshape(batch * seq, d)
    for l in range(depth):
        w = _pad_w_in(w_in[l])
        wk, wvT = _split_w_ukv(w_ukv[l])
        qa, ka, va, qmT, km, vmT, qcT, kc, vcT, kmean = _in_proj(
            xf, row(g_attn[l]), w, row(g_q_lora[l]), row(g_kv_lora[l]), _pad_w_uq(w_uq[l]).T, wk, wvT,
            w[:, C_QC:C_QC + D_MOBA].T, w[:, C_VC:C_VC + D_MOBA].T, cos_t, sin_a, sin_b, cosT, sinT, seq,
            tm=TOKEN_TILE)
        oa = _dilated(qa, ka, va, bias_dil, batch, seq)
        ob = _mla(qmT, km, vmT, batch, seq, tq=MLA_TQ, tk=TOKEN_TILE)
        oc = _moba(qcT, kc, vcT, kmean, bias_moba, batch, seq)
        xf = _post(xf, oa, ob, oc, row(g_mix[l]), w_o[l].astype(BF16), row(g_mlp[l]),
                   w_up[l].astype(BF16), w_down[l].astype(BF16), row(g_final), final=(l == depth - 1))
    return xf.reshape(batch, seq, d)
```

```python
import functools
import math

import jax
import jax.numpy as jnp
from jax import lax
from jax.experimental import pallas as pl
from jax.experimental.pallas import tpu as pltpu

LANES = 128
SUBLANES = 8
HEAD_DIM = 64
H_DIL, H_MLA, H_MOBA = 6, 6, 4
D_DIL = H_DIL * HEAD_DIM
MLA_NOPE, MLA_ROPE, MLA_V = 64, 32, 64
MLA_Q_LORA, MLA_KV_LORA = 384, 128
D_MLA = H_MLA * MLA_V
D_MOBA = H_MOBA * HEAD_DIM
DILATED_PATTERNS = ((128, 1), (512, 4), (2048, 16))
DIL_SPAN = 128
MOBA_BLOCK = 256
MOBA_TOPK = 3
NUM_BUCKETS = 32
MAX_EXACT = 16
REL_MAX_DISTANCE = 2048
ROPE_THETA = 10000.0
EPS = 1e-6
NEG = -1e30
LOG2E = math.log2(math.e)
ONES_ROWS = 16
VMEM_LIMIT = 56 * 1024 * 1024
IN_TILE = 1024
MLA_TQ = 512
MLA_TK = 512

C_QA, C_KA, C_VA = 0, 384, 768
C_CQ, C_CKV, C_KR = 1152, 1536, 1664
C_QC, C_KC, C_VC = 1792, 2048, 2304
D_IN_PAD = 2560

BF16 = jnp.bfloat16
F32 = jnp.float32


def _bucket(dist):
    n = jnp.maximum(dist, 0)
    nf = jnp.maximum(n, 1).astype(F32)
    large = MAX_EXACT + (jnp.log(nf / MAX_EXACT) / math.log(REL_MAX_DISTANCE / MAX_EXACT)
                         * (NUM_BUCKETS - MAX_EXACT)).astype(jnp.int32)
    large = jnp.minimum(large, NUM_BUCKETS - 1)
    return jnp.where(n < MAX_EXACT, n, large)


def _rms(x, g):
    return x * lax.rsqrt(jnp.mean(x * x, axis=-1, keepdims=True) + EPS) * g


def _dot(a, b):
    return jnp.dot(a, b, preferred_element_type=F32)


def _dot_nt(a, b, precision=None):
    return lax.dot_general(a, b, (((1,), (1,)), ((), ())), preferred_element_type=F32,
                           precision=precision)


def _rope(x, cos, sin_a, sin_b):
    return x * cos + pltpu.roll(x, LANES - 16, 1) * sin_a + pltpu.roll(x, 16, 1) * sin_b


def _in_kernel(x_ref, g_ref, w_ref, gq_ref, gkv_ref, wuqT_ref, wk_ref, wvT_ref, wqcT_ref, wvcT_ref,
               cos_ref, sa_ref, sb_ref, cosT_ref, sinT_ref,
               qa_ref, ka_ref, va_ref, qmT_ref, km_ref, vmT_ref, qcT_ref, kc_ref, vcT_ref, kmean_ref):
    hb = _rms(x_ref[...], g_ref[...]).astype(BF16)

    def proj(c0, n):
        return _dot(hb, w_ref[:, c0:c0 + n])

    qa_ref[...] = proj(C_QA, D_DIL) * (HEAD_DIM ** -0.5 * LOG2E)
    ka_ref[...] = proj(C_KA, D_DIL)
    va_ref[...] = proj(C_VA, D_DIL)

    cq = _rms(proj(C_CQ, MLA_Q_LORA), gq_ref[...]).astype(BF16)
    qT = _dot_nt(wuqT_ref[...], cq)
    cosT, sinT = cosT_ref[...], sinT_ref[...]
    scale = (MLA_NOPE + MLA_ROPE) ** -0.5 * LOG2E
    half = MLA_ROPE // 2
    for h in range(H_MLA):
        r0 = h * LANES
        x1 = qT[r0 + MLA_NOPE:r0 + MLA_NOPE + half]
        x2 = qT[r0 + MLA_NOPE + half:r0 + MLA_NOPE + MLA_ROPE]
        roped = jnp.concatenate([qT[r0:r0 + MLA_NOPE], x1 * cosT - x2 * sinT, x2 * cosT + x1 * sinT,
                                 qT[r0 + MLA_NOPE + MLA_ROPE:r0 + LANES]], axis=0)
        qmT_ref[r0:r0 + LANES, :] = (roped * scale).astype(BF16)
    ckv = _rms(proj(C_CKV, MLA_KV_LORA), gkv_ref[...]).astype(BF16)
    kn = _dot(ckv, wk_ref[...])
    vmT = _dot_nt(wvT_ref[...], ckv).astype(BF16)
    for i in range(vmT_ref.shape[0]):
        vmT_ref[i] = vmT[:, i * MLA_TK:(i + 1) * MLA_TK]
    kr = _rope(proj(C_KR, LANES), cos_ref[...], sa_ref[...], sb_ref[...])
    for h in range(H_MLA):
        sl = slice(h * LANES, (h + 1) * LANES)
        km_ref[:, sl] = (kn[:, sl] + kr).astype(BF16)

    qcT_ref[...] = _dot_nt(wqcT_ref[...], hb) * (HEAD_DIM ** -0.5 * LOG2E)
    kc = proj(C_KC, D_MOBA)
    kc_ref[...] = kc.astype(BF16)
    vcT = _dot_nt(wvcT_ref[...], hb).astype(BF16)
    for i in range(kc.shape[0] // MOBA_BLOCK):
        sl = slice(i * MOBA_BLOCK, (i + 1) * MOBA_BLOCK)
        vcT_ref[i] = vcT[:, sl]
        kmean_ref[i] = jnp.mean(kc[sl], axis=0, keepdims=True)


def _in_proj(x, layer, g, w, gq, gkv, wuqT, wk, wvT, wqcT, wvcT, cos, sin_a, sin_b, cosT, sinT, seq, tm):
    t, d = x.shape
    nt = seq // tm
    row = lambda i: (i, 0)
    col = lambda i: (0, i)
    lead = lambda i: (i, 0, 0)
    pos = lambda i: (i % nt, 0)
    posT = lambda i: (0, i % nt)
    full = lambda a: pl.BlockSpec((None,) + a.shape[1:], lambda i: (layer, 0, 0), pipeline_mode=pl.Buffered(1))
    nb = tm // MOBA_BLOCK
    outs = [
        ((t, D_DIL), F32, (tm, D_DIL), row), ((t, D_DIL), F32, (tm, D_DIL), row),
        ((t, D_DIL), F32, (tm, D_DIL), row),
        ((H_MLA * LANES, t), BF16, (H_MLA * LANES, tm), col),
        ((t, H_MLA * LANES), BF16, (tm, H_MLA * LANES), row),
        ((t // MLA_TK, D_MLA, MLA_TK), BF16, (tm // MLA_TK, D_MLA, MLA_TK), lead),
        ((D_MOBA, t), F32, (D_MOBA, tm), col),
        ((t, D_MOBA), BF16, (tm, D_MOBA), row),
        ((t // MOBA_BLOCK, D_MOBA, MOBA_BLOCK), BF16, (nb, D_MOBA, MOBA_BLOCK), lead),
        ((t // MOBA_BLOCK, 1, D_MOBA), F32, (nb, 1, D_MOBA), lead),
    ]
    return pl.pallas_call(
        _in_kernel,
        grid=(t // tm,),
        in_specs=[pl.BlockSpec((tm, d), row), full(g), full(w), full(gq), full(gkv), full(wuqT), full(wk),
                  full(wvT), full(wqcT), full(wvcT),
                  pl.BlockSpec((tm, LANES), pos), pl.BlockSpec((tm, LANES), pos), pl.BlockSpec((tm, LANES), pos),
                  pl.BlockSpec((MLA_ROPE // 2, tm), posT), pl.BlockSpec((MLA_ROPE // 2, tm), posT)],
        out_specs=[pl.BlockSpec(blk, imap) for _, _, blk, imap in outs],
        out_shape=[jax.ShapeDtypeStruct(shape, dt) for shape, dt, _, _ in outs],
        compiler_params=pltpu.CompilerParams(dimension_semantics=("parallel",),
                                             vmem_limit_bytes=VMEM_LIMIT),
        name="in_proj",
    )(x, g, w, gq, gkv, wuqT, wk, wvT, wqcT, wvcT, cos, sin_a, sin_b, cosT, sinT)


def _dil_kernel(q_ref, k_ref, v_ref, bias_ref, o_ref, oscr, lscr, *, unroll):
    seq = q_ref.shape[0]
    span = DIL_SPAN
    lane = lax.broadcasted_iota(jnp.int32, (span, LANES), 1)
    head0 = lane < HEAD_DIM

    for pi, (_, dil) in enumerate(DILATED_PATTERNS):
        nblk = seq // (span * dil)

        def rows(start, dil=dil):
            return pl.ds(start, span) if dil == 1 else pl.ds(start, span, stride=dil)

        def unit(u, carry, first, pi=pi, dil=dil, rows=rows):
            r = u if first else u % dil
            cur = r if first else (u // dil) * (span * dil) + r
            q = q_ref[rows(cur), :]
            qs = jnp.concatenate([jnp.where(head0, q, 0.0), jnp.where(head0, 0.0, q)], axis=0).astype(BF16)
            if first:
                kk = k_ref[rows(cur), :].astype(BF16)
                vv = v_ref[rows(cur), :].astype(BF16)
                s = _dot_nt(qs, kk) + bias_ref[pi, :, span:]
            else:
                prev = cur - span * dil
                kk = jnp.concatenate([k_ref[rows(prev), :], k_ref[rows(cur), :]], axis=0).astype(BF16)
                vv = jnp.concatenate([v_ref[rows(prev), :], v_ref[rows(cur), :]], axis=0).astype(BF16)
                s = _dot_nt(qs, kk) + bias_ref[pi]
            m = jnp.max(s, axis=-1, keepdims=True)
            p = jnp.exp2(s - m).astype(BF16)
            on = _dot(p, jnp.concatenate([vv, jnp.ones_like(vv)], axis=1))
            l = on[:, LANES:]
            o = on[:, :LANES] / l
            lse = m + jnp.log(l) * LOG2E
            oscr[pi, rows(cur), :] = jnp.where(head0, o[:span], o[span:])
            lscr[pi, rows(cur), :] = jnp.where(head0, lse[:span], lse[span:])
            return carry

        lax.fori_loop(0, dil, functools.partial(unit, first=True), 0, unroll=min(unroll, dil))
        lax.fori_loop(dil, nblk * dil, functools.partial(unit, first=False), 0, unroll=unroll)

    ct = 512
    for c in range(seq // ct):
        sl = pl.ds(c * ct, ct)
        l0, l1, l2 = lscr[0, sl, :], lscr[1, sl, :], lscr[2, sl, :]
        mx = jnp.maximum(jnp.maximum(l0, l1), l2)
        e0, e1, e2 = jnp.exp2(l0 - mx), jnp.exp2(l1 - mx), jnp.exp2(l2 - mx)
        num = e0 * oscr[0, sl, :] + e1 * oscr[1, sl, :] + e2 * oscr[2, sl, :]
        o_ref[sl, :] = num / (e0 + e1 + e2)


def _dilated(qa, ka, va, bias, batch, seq, unroll=8):
    t = qa.shape[0]
    npair = H_DIL // 2
    blk = pl.BlockSpec((seq, LANES), lambda b, p: (b, p))
    return pl.pallas_call(
        functools.partial(_dil_kernel, unroll=unroll),
        grid=(batch, npair),
        in_specs=[blk, blk, blk,
                  pl.BlockSpec((None, len(DILATED_PATTERNS), 2 * DIL_SPAN, 2 * DIL_SPAN),
                               lambda b, p: (p, 0, 0, 0))],
        out_specs=blk,
        out_shape=jax.ShapeDtypeStruct((t, D_DIL), F32),
        scratch_shapes=[pltpu.VMEM((len(DILATED_PATTERNS), seq, LANES), F32),
                        pltpu.VMEM((len(DILATED_PATTERNS), seq, LANES), F32)],
        compiler_params=pltpu.CompilerParams(dimension_semantics=("parallel", "parallel"),
                                             vmem_limit_bytes=VMEM_LIMIT),
        name="dilated",
    )(qa, ka, va, bias)


def _mla_kernel(qT_ref, k_ref, vT_ref, o_ref, *, tq, tk, nsub):
    i = pl.program_id(2)
    sub = tk // nsub
    krow = lax.broadcasted_iota(jnp.int32, (tk, 2 * tq), 0)
    qcol = lax.broadcasted_iota(jnp.int32, (tk, 2 * tq), 1)
    qcol = jnp.where(qcol >= tq, qcol - tq, qcol)
    nfull = (i * tq) // tk
    causal = krow <= qcol + (i * tq - nfull * tk)
    zero = jnp.zeros((LANES, tq), BF16)
    rhs = jnp.concatenate([jnp.concatenate([qT_ref[:LANES, :], zero], axis=1),
                           jnp.concatenate([zero, qT_ref[LANES:, :]], axis=1)], axis=0)

    ones = jnp.ones((ONES_ROWS, sub), BF16)

    def step(j, carry, masked):
        m, acc = carry
        vT = vT_ref[j]
        ss = []
        for u in range(nsub):
            s = _dot(k_ref[pl.ds(pl.multiple_of(j * tk + u * sub, sub), sub), :], rhs)
            if masked:
                s = jnp.where(causal[u * sub:(u + 1) * sub], s, NEG)
            ss.append(s)
        for u, s in enumerate(ss):
            m_new = jnp.maximum(m, jnp.max(s, axis=0, keepdims=True))
            p = jnp.exp2(s - m_new).astype(BF16)
            vu = vT[:, u * sub:(u + 1) * sub]
            pv = jnp.concatenate([_dot(jnp.concatenate([vu[:MLA_V], ones], axis=0), p[:, :tq]),
                                  _dot(jnp.concatenate([vu[MLA_V:], ones], axis=0), p[:, tq:])], axis=1)
            acc = jnp.exp2(m - m_new) * acc + pv
            m = m_new
        return m, acc

    init = (jnp.full((1, 2 * tq), NEG, F32), jnp.zeros((MLA_V + ONES_ROWS, 2 * tq), F32))
    carry = lax.fori_loop(0, nfull, functools.partial(step, masked=False), init)
    _, acc = step(nfull, carry, masked=True)
    oT = acc[:MLA_V] / acc[MLA_V:MLA_V + 1]
    o_ref[...] = jnp.concatenate([oT[:, :tq], oT[:, tq:]], axis=0).T


def _mla(qmT, km, vmT, batch, seq, tq, tk, nsub=2):
    t = km.shape[0]
    nq = seq // tq
    npair = H_MLA // 2
    assert tk % tq == 0 and seq % tk == 0
    return pl.pallas_call(
        functools.partial(_mla_kernel, tq=tq, tk=tk, nsub=nsub),
        grid=(batch, npair, nq),
        in_specs=[pl.BlockSpec((2 * LANES, tq), lambda b, p, i: (p, b * nq + i)),
                  pl.BlockSpec((seq, 2 * LANES), lambda b, p, i: (b, p)),
                  pl.BlockSpec((seq // tk, LANES, tk), lambda b, p, i: (b, p, 0))],
        out_specs=pl.BlockSpec((tq, LANES), lambda b, p, i: (b * nq + i, p)),
        out_shape=jax.ShapeDtypeStruct((t, D_MLA), F32),
        compiler_params=pltpu.CompilerParams(dimension_semantics=("parallel", "parallel", "arbitrary"),
                                             vmem_limit_bytes=VMEM_LIMIT),
        name="mla",
    )(qmT, km, vmT)


def _moba_kernel(qT_ref, k_ref, vT_ref, kmean_ref, bias_ref, o_ref, pen_ref, *, nfar):
    blk = MOBA_BLOCK
    wide = H_MOBA * blk
    j = pl.program_id(1)
    nblk = kmean_ref.shape[0]
    qT = qT_ref[...]
    rhead = lax.broadcasted_iota(jnp.int32, (D_MOBA, blk), 0) // HEAD_DIM
    qsT = jnp.concatenate([jnp.where(rhead == h, qT, 0.0) for h in range(H_MOBA)], axis=1)

    gate = jnp.dot(kmean_ref[:, 0, :], qsT, preferred_element_type=F32, precision=lax.Precision.HIGHEST)
    bidx = lax.broadcasted_iota(jnp.int32, (nblk, wide), 0)
    gate = jnp.where(bidx < j, gate, -jnp.inf)
    pen = jnp.full((nblk, wide), NEG, F32)
    for _ in range(MOBA_TOPK):
        mx = jnp.max(gate, axis=0, keepdims=True)
        first = jnp.min(jnp.where(gate == mx, bidx, nblk), axis=0, keepdims=True)
        hit = bidx == first
        pen = jnp.where(hit & (mx > -jnp.inf), 0.0, pen)
        gate = jnp.where(hit, -jnp.inf, gate)

    pen_ref[0:nblk, :] = pen
    pen_ref[nblk:nblk + 1, :] = jnp.zeros((1, wide), F32)
    pen_ref[nblk + 1:nblk + 2, :] = jnp.full((1, wide), NEG, F32)
    rhs = qsT.astype(BF16)

    def scores(n, pick):
        s = _dot(k_ref[pl.ds(pl.multiple_of(n * blk, blk), blk), :], rhs)
        return s + bias_ref[jnp.minimum(j - n, nfar)] + pen_ref[pl.ds(pick, 1), :]

    ones = jnp.ones((ONES_ROWS, blk), BF16)

    def weighted(n, p):
        vT = vT_ref[n]
        return jnp.concatenate(
            [_dot(jnp.concatenate([vT[h * HEAD_DIM:(h + 1) * HEAD_DIM], ones], axis=0), p[:, h * blk:(h + 1) * blk])
             for h in range(H_MOBA)], axis=1)

    def pair(t, carry):
        m, acc = carry
        n0 = j - 2 * t
        n1 = jnp.maximum(n0 - 1, 0)
        s0 = scores(n0, jnp.where(t == 0, nblk, n0))
        s1 = scores(n1, jnp.where(n0 == 0, nblk + 1, n1))
        for n, s in ((n0, s0), (n1, s1)):
            m_new = jnp.maximum(m, jnp.max(s, axis=0, keepdims=True))
            p = jnp.exp2(s - m_new).astype(BF16)
            acc = jnp.exp2(m - m_new) * acc + weighted(n, p)
            m = m_new
        return m, acc

    init = (jnp.full((1, wide), NEG, F32), jnp.zeros((HEAD_DIM + ONES_ROWS, wide), F32))
    _, acc = lax.fori_loop(0, j // 2 + 1, pair, init)
    oT = acc[:HEAD_DIM] / acc[HEAD_DIM:HEAD_DIM + 1]
    o_ref[...] = jnp.concatenate([oT[:, h * blk:(h + 1) * blk] for h in range(H_MOBA)], axis=0).T


def _moba(qcT, kc, vcT, kmean, bias, batch, seq):
    t = kc.shape[0]
    nblk = seq // MOBA_BLOCK
    nfar = bias.shape[0] - 1
    return pl.pallas_call(
        functools.partial(_moba_kernel, nfar=nfar),
        grid=(batch, nblk),
        in_specs=[pl.BlockSpec((D_MOBA, MOBA_BLOCK), lambda b, j: (0, b * nblk + j)),
                  pl.BlockSpec((seq, D_MOBA), lambda b, j: (b, 0)),
                  pl.BlockSpec((nblk, D_MOBA, MOBA_BLOCK), lambda b, j: (b, 0, 0)),
                  pl.BlockSpec((nblk, 1, D_MOBA), lambda b, j: (b, 0, 0)),
                  pl.BlockSpec(bias.shape, lambda b, j: (0, 0, 0), pipeline_mode=pl.Buffered(1))],
        out_specs=pl.BlockSpec((MOBA_BLOCK, D_MOBA), lambda b, j: (b * nblk + j, 0)),
        out_shape=jax.ShapeDtypeStruct((t, D_MOBA), F32),
        scratch_shapes=[pltpu.VMEM((nblk + SUBLANES, H_MOBA * MOBA_BLOCK), F32)],
        compiler_params=pltpu.CompilerParams(dimension_semantics=("parallel", "arbitrary"),
                                             vmem_limit_bytes=VMEM_LIMIT),
        name="moba",
    )(qcT, kc, vcT, kmean, bias)


def _post_kernel(x_ref, oa_ref, ob_ref, oc_ref, gmix_ref, wo_ref, gmlp_ref, wup_ref, wdn_ref, gfin_ref,
                 y_ref, *, final, ff_chunk):
    gm = gmix_ref[...]
    x = x_ref[...]
    lo = 0
    for o_ref in (oa_ref, ob_ref, oc_ref):
        n = o_ref.shape[1]
        mixed = _rms(o_ref[...], gm[:, lo:lo + n]).astype(BF16)
        x = x + _dot(mixed, wo_ref[lo:lo + n, :])
        lo += n
    hb = _rms(x, gmlp_ref[...]).astype(BF16)
    y_ref[...] = x
    for c in range(wup_ref.shape[1] // ff_chunk):
        cs = slice(c * ff_chunk, (c + 1) * ff_chunk)
        u = jnp.maximum(_dot(hb, wup_ref[:, cs]), 0.0)
        y_ref[...] += _dot((u * u).astype(BF16), wdn_ref[cs, :])
    if final:
        y_ref[...] = _rms(y_ref[...], gfin_ref[...])


def _post(x, oa, ob, oc, layer, gmix, wo, gmlp, wup, wdn, gfin, final, tm=512, ff_chunk=1024):
    t, d = x.shape
    row = lambda i: (i, 0)
    stacked = lambda a: pl.BlockSpec((None,) + a.shape[1:], lambda i: (layer, 0, 0), pipeline_mode=pl.Buffered(1))
    return pl.pallas_call(
        functools.partial(_post_kernel, final=final, ff_chunk=ff_chunk),
        grid=(t // tm,),
        in_specs=[pl.BlockSpec((tm, d), row), pl.BlockSpec((tm, oa.shape[1]), row),
                  pl.BlockSpec((tm, ob.shape[1]), row), pl.BlockSpec((tm, oc.shape[1]), row),
                  stacked(gmix), stacked(wo), stacked(gmlp), stacked(wup), stacked(wdn),
                  pl.BlockSpec(gfin.shape, lambda i: (0, 0))],
        out_specs=pl.BlockSpec((tm, d), row),
        out_shape=jax.ShapeDtypeStruct((t, d), F32),
        compiler_params=pltpu.CompilerParams(dimension_semantics=("parallel",),
                                             vmem_limit_bytes=VMEM_LIMIT),
        name="post",
    )(x, oa, ob, oc, gmix, wo, gmlp, wup, wdn, gfin)


def _rope_tables(seq):
    inv_freq = ROPE_THETA ** (-jnp.arange(0, MLA_ROPE, 2, dtype=F32) / MLA_ROPE)
    ang = jnp.arange(seq, dtype=F32)[:, None] * inv_freq[None, :]
    cos, sin = jnp.cos(ang), jnp.sin(ang)
    half = MLA_ROPE // 2
    one = jnp.ones((seq, MLA_NOPE), F32)
    zero = jnp.zeros((seq, MLA_NOPE), F32)
    zh = jnp.zeros((seq, half), F32)
    tail1 = jnp.ones((seq, LANES - MLA_NOPE - MLA_ROPE), F32)
    tail0 = jnp.zeros((seq, LANES - MLA_NOPE - MLA_ROPE), F32)
    cos_t = jnp.concatenate([one, cos, cos, tail1], axis=1)
    sin_a = jnp.concatenate([zero, -sin, zh, tail0], axis=1)
    sin_b = jnp.concatenate([zero, zh, sin, tail0], axis=1)
    return cos_t, sin_a, sin_b, cos.T, sin.T


def _lookup(tab, bucket):
    col = lambda b: tab[b][(slice(None),) + (None,) * bucket.ndim]
    out = jnp.broadcast_to(col(0), (tab.shape[1],) + bucket.shape)
    for b in range(1, NUM_BUCKETS):
        out = jnp.where(bucket[None] == b, col(b), out)
    return out.astype(F32)


def _dil_bias(bias_tab):
    span = DIL_SPAN
    npat = len(DILATED_PATTERNS)
    diff = span + jnp.arange(span)[:, None] - jnp.arange(2 * span)[None, :]
    in_band = (diff >= 0) & (diff <= span)
    tabs = []
    for _, dil in DILATED_PATTERNS:
        tabs.append(jnp.where(in_band[None], _lookup(bias_tab, _bucket(diff * dil)) * LOG2E, NEG))
    b = jnp.stack(tabs, axis=1)
    b = b.reshape(H_DIL // 2, 2, npat, span, 2 * span).transpose(0, 2, 1, 3, 4)
    return b.reshape(H_DIL // 2, npat, 2 * span, 2 * span)


def _moba_bias(bias_tab, nblk):
    blk = MOBA_BLOCK
    nfar = min(nblk - 1, REL_MAX_DISTANCE // blk + 1)
    delta = jnp.arange(nfar + 1)[:, None, None] * blk
    dist = delta + jnp.arange(blk)[None, :, None] - jnp.arange(blk)[None, None, :]
    b = jnp.where(dist[None] >= 0, _lookup(bias_tab, _bucket(dist)) * LOG2E, NEG)
    return b.transpose(1, 3, 0, 2).reshape(nfar + 1, blk, H_MOBA * blk)


def _pad_w_in(w):
    sizes = (D_DIL, D_DIL, D_DIL, MLA_Q_LORA, MLA_KV_LORA, MLA_ROPE, D_MOBA, D_MOBA, D_MOBA)
    parts, lo = [], 0
    for n in sizes:
        parts.append(w[..., lo:lo + n])
        lo += n
    zeros = lambda n: jnp.zeros(w.shape[:-1] + (n,), w.dtype)
    parts[5] = jnp.concatenate([zeros(MLA_NOPE), parts[5], zeros(LANES - MLA_NOPE - MLA_ROPE)], axis=-1)
    return jnp.concatenate(parts, axis=-1).astype(BF16)


def _pad_w_uq_t(w):
    depth, r, _ = w.shape
    w = w.reshape(depth, r, H_MLA, MLA_NOPE + MLA_ROPE)
    w = jnp.pad(w, ((0, 0), (0, 0), (0, 0), (0, LANES - MLA_NOPE - MLA_ROPE)))
    return w.reshape(depth, r, H_MLA * LANES).transpose(0, 2, 1).astype(BF16)


def _split_w_ukv(w):
    depth, r, _ = w.shape
    w = w.reshape(depth, r, H_MLA, MLA_NOPE + MLA_V)
    wk = jnp.pad(w[..., :MLA_NOPE], ((0, 0), (0, 0), (0, 0), (0, LANES - MLA_NOPE)))
    wv = w[..., MLA_NOPE:].reshape(depth, r, H_MLA * MLA_V)
    return wk.reshape(depth, r, H_MLA * LANES).astype(BF16), wv.transpose(0, 2, 1).astype(BF16)


def kernel(x, g_attn, w_in, g_q_lora, g_kv_lora, w_uq, w_ukv, rel_bias, g_mix, w_o, g_mlp, w_up, w_down,
           g_final):
    batch, seq, d = x.shape
    depth = w_in.shape[0]
    assert seq % (DIL_SPAN * max(dil for _, dil in DILATED_PATTERNS)) == 0
    assert seq % IN_TILE == 0 and IN_TILE % MLA_TK == 0
    cos_t, sin_a, sin_b, cosT, sinT = _rope_tables(seq)
    bias_dil = _dil_bias(rel_bias[:, :H_DIL])
    bias_moba = _moba_bias(rel_bias[:, H_DIL:], seq // MOBA_BLOCK)
    rows = lambda v: v[:, None, :].astype(F32)
    w = _pad_w_in(w_in)
    wk, wvT = _split_w_ukv(w_ukv)
    in_params = (rows(g_attn), w, rows(g_q_lora), rows(g_kv_lora), _pad_w_uq_t(w_uq), wk, wvT,
                 w[:, :, C_QC:C_QC + D_MOBA].transpose(0, 2, 1), w[:, :, C_VC:C_VC + D_MOBA].transpose(0, 2, 1))
    post_params = (rows(g_mix), w_o.astype(BF16), rows(g_mlp), w_up.astype(BF16), w_down.astype(BF16))
    gfin = g_final.reshape(1, -1).astype(F32)
    xf = x.reshape(batch * seq, d)
    for l in range(depth):
        qa, ka, va, qmT, km, vmT, qcT, kc, vcT, kmean = _in_proj(
            xf, l, *in_params, cos_t, sin_a, sin_b, cosT, sinT, seq, tm=IN_TILE)
        oa = _dilated(qa, ka, va, bias_dil, batch, seq)
        ob = _mla(qmT, km, vmT, batch, seq, tq=MLA_TQ, tk=MLA_TK)
        oc = _moba(qcT, kc, vcT, kmean, bias_moba, batch, seq)
        xf = _post(xf, oa, ob, oc, l, *post_params, gfin, final=(l == depth - 1))
    return xf.reshape(batch, seq, d)
```

```python
import functools
import math

import jax
import jax.numpy as jnp
from jax import lax
from jax.experimental import pallas as pl
from jax.experimental.pallas import tpu as pltpu

LANES = 128
SUBLANES = 8
HEAD_DIM = 64
H_DIL, H_MLA, H_MOBA = 6, 6, 4
D_DIL = H_DIL * HEAD_DIM
MLA_NOPE, MLA_ROPE, MLA_V = 64, 32, 64
MLA_Q_LORA, MLA_KV_LORA = 384, 128
D_MLA = H_MLA * MLA_V
D_MOBA = H_MOBA * HEAD_DIM
DILATED_PATTERNS = ((128, 1), (512, 4), (2048, 16))
DIL_SPAN = 128
MOBA_BLOCK = 256
MOBA_TOPK = 3
NUM_BUCKETS = 32
MAX_EXACT = 16
REL_MAX_DISTANCE = 2048
ROPE_THETA = 10000.0
EPS = 1e-6
NEG = -1e30
LOG2E = math.log2(math.e)
ONES_ROWS = 16
VMEM_LIMIT = 56 * 1024 * 1024
IN_TILE = 1024
MLA_TQ = 512
MLA_TK = 512

C_QA, C_KA, C_VA = 0, 384, 768
C_CQ, C_CKV, C_KR = 1152, 1536, 1664
C_QC, C_KC, C_VC = 1792, 2048, 2304
D_IN_PAD = 2560

BF16 = jnp.bfloat16
F32 = jnp.float32


def _bucket(dist):
    n = jnp.maximum(dist, 0)
    nf = jnp.maximum(n, 1).astype(F32)
    large = MAX_EXACT + (jnp.log(nf / MAX_EXACT) / math.log(REL_MAX_DISTANCE / MAX_EXACT)
                         * (NUM_BUCKETS - MAX_EXACT)).astype(jnp.int32)
    large = jnp.minimum(large, NUM_BUCKETS - 1)
    return jnp.where(n < MAX_EXACT, n, large)


def _rms(x, g):
    return x * lax.rsqrt(jnp.mean(x * x, axis=-1, keepdims=True) + EPS) * g


def _dot(a, b):
    return jnp.dot(a, b, preferred_element_type=F32)


def _dot_nt(a, b, precision=None):
    return lax.dot_general(a, b, (((1,), (1,)), ((), ())), preferred_element_type=F32,
                           precision=precision)


def _rope(x, cos, sin_a, sin_b):
    return x * cos + pltpu.roll(x, LANES - 16, 1) * sin_a + pltpu.roll(x, 16, 1) * sin_b


def _in_kernel(x_ref, g_ref, w_ref, gq_ref, gkv_ref, wuqT_ref, wk_ref, wvT_ref, wqcT_ref, wvcT_ref,
               cos_ref, sa_ref, sb_ref, cosT_ref, sinT_ref,
               qa_ref, ka_ref, va_ref, qmT_ref, km_ref, vmT_ref, qcT_ref, kc_ref, vcT_ref, kmean_ref):
    hb = _rms(x_ref[...], g_ref[...]).astype(BF16)

    def proj(c0, n):
        return _dot(hb, w_ref[:, c0:c0 + n])

    qa_ref[...] = proj(C_QA, D_DIL) * (HEAD_DIM ** -0.5 * LOG2E)
    ka_ref[...] = proj(C_KA, D_DIL)
    va_ref[...] = proj(C_VA, D_DIL)

    cq = _rms(proj(C_CQ, MLA_Q_LORA), gq_ref[...]).astype(BF16)
    qT = _dot_nt(wuqT_ref[...], cq)
    cosT, sinT = cosT_ref[...], sinT_ref[...]
    scale = (MLA_NOPE + MLA_ROPE) ** -0.5 * LOG2E
    half = MLA_ROPE // 2
    for h in range(H_MLA):
        r0 = h * LANES
        x1 = qT[r0 + MLA_NOPE:r0 + MLA_NOPE + half]
        x2 = qT[r0 + MLA_NOPE + half:r0 + MLA_NOPE + MLA_ROPE]
        roped = jnp.concatenate([qT[r0:r0 + MLA_NOPE], x1 * cosT - x2 * sinT, x2 * cosT + x1 * sinT,
                                 qT[r0 + MLA_NOPE + MLA_ROPE:r0 + LANES]], axis=0)
        qmT_ref[r0:r0 + LANES, :] = (roped * scale).astype(BF16)
    ckv = _rms(proj(C_CKV, MLA_KV_LORA), gkv_ref[...]).astype(BF16)
    kn = _dot(ckv, wk_ref[...])
    vmT = _dot_nt(wvT_ref[...], ckv).astype(BF16)
    for i in range(vmT_ref.shape[0]):
        vmT_ref[i] = vmT[:, i * MLA_TK:(i + 1) * MLA_TK]
    kr = _rope(proj(C_KR, LANES), cos_ref[...], sa_ref[...], sb_ref[...])
    for h in range(H_MLA):
        sl = slice(h * LANES, (h + 1) * LANES)
        km_ref[:, sl] = (kn[:, sl] + kr).astype(BF16)

    qcT_ref[...] = _dot_nt(wqcT_ref[...], hb) * (HEAD_DIM ** -0.5 * LOG2E)
    kc = proj(C_KC, D_MOBA)
    kc_ref[...] = kc.astype(BF16)
    vcT = _dot_nt(wvcT_ref[...], hb).astype(BF16)
    for i in range(kc.shape[0] // MOBA_BLOCK):
        sl = slice(i * MOBA_BLOCK, (i + 1) * MOBA_BLOCK)
        vcT_ref[i] = vcT[:, sl]
        kmean_ref[i] = jnp.mean(kc[sl], axis=0, keepdims=True)


def _in_proj(x, layer, g, w, gq, gkv, wuqT, wk, wvT, wqcT, wvcT, cos, sin_a, sin_b, cosT, sinT, seq, tm):
    t, d = x.shape
    nt = seq // tm
    row = lambda i: (i, 0)
    col = lambda i: (0, i)
    lead = lambda i: (i, 0, 0)
    pos = lambda i: (i % nt, 0)
    posT = lambda i: (0, i % nt)
    full = lambda a: pl.BlockSpec((None,) + a.shape[1:], lambda i: (layer, 0, 0), pipeline_mode=pl.Buffered(1))
    nb = tm // MOBA_BLOCK
    outs = [
        ((t, D_DIL), F32, (tm, D_DIL), row), ((t, D_DIL), F32, (tm, D_DIL), row),
        ((t, D_DIL), F32, (tm, D_DIL), row),
        ((H_MLA * LANES, t), BF16, (H_MLA * LANES, tm), col),
        ((t, H_MLA * LANES), BF16, (tm, H_MLA * LANES), row),
        ((t // MLA_TK, D_MLA, MLA_TK), BF16, (tm // MLA_TK, D_MLA, MLA_TK), lead),
        ((D_MOBA, t), F32, (D_MOBA, tm), col),
        ((t, D_MOBA), BF16, (tm, D_MOBA), row),
        ((t // MOBA_BLOCK, D_MOBA, MOBA_BLOCK), BF16, (nb, D_MOBA, MOBA_BLOCK), lead),
        ((t // MOBA_BLOCK, 1, D_MOBA), F32, (nb, 1, D_MOBA), lead),
    ]
    return pl.pallas_call(
        _in_kernel,
        grid=(t // tm,),
        in_specs=[pl.BlockSpec((tm, d), row), full(g), full(w), full(gq), full(gkv), full(wuqT), full(wk),
                  full(wvT), full(wqcT), full(wvcT),
                  pl.BlockSpec((tm, LANES), pos), pl.BlockSpec((tm, LANES), pos), pl.BlockSpec((tm, LANES), pos),
                  pl.BlockSpec((MLA_ROPE // 2, tm), posT), pl.BlockSpec((MLA_ROPE // 2, tm), posT)],
        out_specs=[pl.BlockSpec(blk, imap) for _, _, blk, imap in outs],
        out_shape=[jax.ShapeDtypeStruct(shape, dt) for shape, dt, _, _ in outs],
        compiler_params=pltpu.CompilerParams(dimension_semantics=("parallel",),
                                             vmem_limit_bytes=VMEM_LIMIT),
        name="in_proj",
    )(x, g, w, gq, gkv, wuqT, wk, wvT, wqcT, wvcT, cos, sin_a, sin_b, cosT, sinT)


def _dil_kernel(q_ref, k_ref, v_ref, bias_ref, o_ref, oscr, lscr, *, unroll):
    seq = q_ref.shape[0]
    span = DIL_SPAN
    lane = lax.broadcasted_iota(jnp.int32, (span, LANES), 1)
    head0 = lane < HEAD_DIM

    for pi, (_, dil) in enumerate(DILATED_PATTERNS):
        nblk = seq // (span * dil)

        def rows(start, dil=dil):
            return pl.ds(start, span) if dil == 1 else pl.ds(start, span, stride=dil)

        def unit(u, carry, first, pi=pi, dil=dil, rows=rows):
            r = u if first else u % dil
            cur = r if first else (u // dil) * (span * dil) + r
            q = q_ref[rows(cur), :]
            qs = jnp.concatenate([jnp.where(head0, q, 0.0), jnp.where(head0, 0.0, q)], axis=0).astype(BF16)
            if first:
                kk = k_ref[rows(cur), :].astype(BF16)
                vv = v_ref[rows(cur), :].astype(BF16)
                s = _dot_nt(qs, kk) + bias_ref[pi, :, span:]
            else:
                prev = cur - span * dil
                kk = jnp.concatenate([k_ref[rows(prev), :], k_ref[rows(cur), :]], axis=0).astype(BF16)
                vv = jnp.concatenate([v_ref[rows(prev), :], v_ref[rows(cur), :]], axis=0).astype(BF16)
                s = _dot_nt(qs, kk) + bias_ref[pi]
            m = jnp.max(s, axis=-1, keepdims=True)
            p = jnp.exp2(s - m).astype(BF16)
            on = _dot(p, jnp.concatenate([vv, jnp.ones_like(vv)], axis=1))
            l = on[:, LANES:]
            o = on[:, :LANES] / l
            lse = m + jnp.log(l) * LOG2E
            oscr[pi, rows(cur), :] = jnp.where(head0, o[:span], o[span:])
            lscr[pi, rows(cur), :] = jnp.where(head0, lse[:span], lse[span:])
            return carry

        lax.fori_loop(0, dil, functools.partial(unit, first=True), 0, unroll=min(unroll, dil))
        lax.fori_loop(dil, nblk * dil, functools.partial(unit, first=False), 0, unroll=unroll)

    ct = 512
    for c in range(seq // ct):
        sl = pl.ds(c * ct, ct)
        l0, l1, l2 = lscr[0, sl, :], lscr[1, sl, :], lscr[2, sl, :]
        mx = jnp.maximum(jnp.maximum(l0, l1), l2)
        e0, e1, e2 = jnp.exp2(l0 - mx), jnp.exp2(l1 - mx), jnp.exp2(l2 - mx)
        num = e0 * oscr[0, sl, :] + e1 * oscr[1, sl, :] + e2 * oscr[2, sl, :]
        o_ref[sl, :] = num / (e0 + e1 + e2)


def _dilated(qa, ka, va, bias, batch, seq, unroll=8):
    t = qa.shape[0]
    npair = H_DIL // 2
    blk = pl.BlockSpec((seq, LANES), lambda b, p: (b, p))
    return pl.pallas_call(
        functools.partial(_dil_kernel, unroll=unroll),
        grid=(batch, npair),
        in_specs=[blk, blk, blk,
                  pl.BlockSpec((None, len(DILATED_PATTERNS), 2 * DIL_SPAN, 2 * DIL_SPAN),
                               lambda b, p: (p, 0, 0, 0))],
        out_specs=blk,
        out_shape=jax.ShapeDtypeStruct((t, D_DIL), F32),
        scratch_shapes=[pltpu.VMEM((len(DILATED_PATTERNS), seq, LANES), F32),
                        pltpu.VMEM((len(DILATED_PATTERNS), seq, LANES), F32)],
        compiler_params=pltpu.CompilerParams(dimension_semantics=("parallel", "parallel"),
                                             vmem_limit_bytes=VMEM_LIMIT),
        name="dilated",
    )(qa, ka, va, bias)


def _mla_kernel(qT_ref, k_ref, vT_ref, o_ref, *, tq, tk, nsub):
    i = pl.program_id(2)
    sub = tk // nsub
    zero = jnp.zeros((LANES, tq), BF16)
    rhs = jnp.concatenate([jnp.concatenate([qT_ref[:LANES, :], zero], axis=1),
                           jnp.concatenate([zero, qT_ref[LANES:, :]], axis=1)], axis=0)
    ones = jnp.ones((ONES_ROWS, sub), BF16)

    def both(x, lo):
        return x if lo == 0 else jnp.concatenate([x[:, lo:tq], x[:, tq + lo:]], axis=1)

    def scores(j, u, lo, diag):
        w = tq - lo
        s = _dot(k_ref[pl.ds(pl.multiple_of(j * tk + u * sub, sub), sub), :], both(rhs, lo))
        if diag:
            krow = lax.broadcasted_iota(jnp.int32, (sub, 2 * w), 0) + u * sub
            qcol = lax.broadcasted_iota(jnp.int32, (sub, 2 * w), 1)
            qcol = jnp.where(qcol >= w, qcol - w, qcol) + lo
            s = jnp.where(krow <= qcol, s, NEG)
        return s

    def update(j, u, s, m, acc):
        w = s.shape[1] // 2
        m_new = jnp.maximum(m, jnp.max(s, axis=0, keepdims=True))
        p = jnp.exp2(s - m_new).astype(BF16)
        vu = vT_ref[j][:, u * sub:(u + 1) * sub]
        pv = jnp.concatenate([_dot(jnp.concatenate([vu[:MLA_V], ones], axis=0), p[:, :w]),
                              _dot(jnp.concatenate([vu[MLA_V:], ones], axis=0), p[:, w:])], axis=1)
        return m_new, jnp.exp2(m - m_new) * acc + pv

    def past(j, carry):
        ss = [scores(j, u, 0, False) for u in range(nsub)]
        for u, s in enumerate(ss):
            carry = update(j, u, s, *carry)
        return carry

    init = (jnp.full((1, 2 * tq), NEG, F32), jnp.zeros((MLA_V + ONES_ROWS, 2 * tq), F32))
    m, acc = lax.fori_loop(0, i, past, init)
    ss = [scores(i, u, u * sub, True) for u in range(nsub)]
    for u, s in enumerate(ss):
        lo = u * sub
        m_u, acc_u = update(i, u, s, both(m, lo), both(acc, lo))
        if lo:
            m = jnp.concatenate([m[:, :lo], m_u[:, :tq - lo], m[:, tq:tq + lo], m_u[:, tq - lo:]], axis=1)
            acc = jnp.concatenate([acc[:, :lo], acc_u[:, :tq - lo], acc[:, tq:tq + lo], acc_u[:, tq - lo:]], axis=1)
        else:
            m, acc = m_u, acc_u
    oT = acc[:MLA_V] / acc[MLA_V:MLA_V + 1]
    o_ref[...] = jnp.concatenate([oT[:, :tq], oT[:, tq:]], axis=0).T


def _mla(qmT, km, vmT, batch, seq, tq, tk, nsub=2):
    t = km.shape[0]
    nq = seq // tq
    npair = H_MLA // 2
    assert tk == tq and seq % tk == 0
    return pl.pallas_call(
        functools.partial(_mla_kernel, tq=tq, tk=tk, nsub=nsub),
        grid=(batch, npair, nq),
        in_specs=[pl.BlockSpec((2 * LANES, tq), lambda b, p, i: (p, b * nq + i)),
                  pl.BlockSpec((seq, 2 * LANES), lambda b, p, i: (b, p)),
                  pl.BlockSpec((seq // tk, LANES, tk), lambda b, p, i: (b, p, 0))],
        out_specs=pl.BlockSpec((tq, LANES), lambda b, p, i: (b * nq + i, p)),
        out_shape=jax.ShapeDtypeStruct((t, D_MLA), F32),
        compiler_params=pltpu.CompilerParams(dimension_semantics=("parallel", "parallel", "arbitrary"),
                                             vmem_limit_bytes=VMEM_LIMIT),
        name="mla",
    )(qmT, km, vmT)


def _moba_kernel(qT_ref, k_ref, vT_ref, kmean_ref, bias_ref, o_ref, pen_ref, *, nfar):
    blk = MOBA_BLOCK
    wide = H_MOBA * blk
    j = pl.program_id(1)
    nblk = kmean_ref.shape[0]
    qT = qT_ref[...]
    rhead = lax.broadcasted_iota(jnp.int32, (D_MOBA, blk), 0) // HEAD_DIM
    qsT = jnp.concatenate([jnp.where(rhead == h, qT, 0.0) for h in range(H_MOBA)], axis=1)

    gate = jnp.dot(kmean_ref[:, 0, :], qsT, preferred_element_type=F32, precision=lax.Precision.HIGHEST)
    bidx = lax.broadcasted_iota(jnp.int32, (nblk, wide), 0)
    gate = jnp.where(bidx < j, gate, -jnp.inf)
    pen = jnp.full((nblk, wide), NEG, F32)
    for _ in range(MOBA_TOPK):
        mx = jnp.max(gate, axis=0, keepdims=True)
        first = jnp.min(jnp.where(gate == mx, bidx, nblk), axis=0, keepdims=True)
        hit = bidx == first
        pen = jnp.where(hit & (mx > -jnp.inf), 0.0, pen)
        gate = jnp.where(hit, -jnp.inf, gate)

    pen_ref[0:nblk, :] = pen
    pen_ref[nblk:nblk + 1, :] = jnp.zeros((1, wide), F32)
    pen_ref[nblk + 1:nblk + 2, :] = jnp.full((1, wide), NEG, F32)
    rhs = qsT.astype(BF16)

    def scores(n, pick):
        s = _dot(k_ref[pl.ds(pl.multiple_of(n * blk, blk), blk), :], rhs)
        return s + bias_ref[jnp.minimum(j - n, nfar)] + pen_ref[pl.ds(pick, 1), :]

    ones = jnp.ones((ONES_ROWS, blk), BF16)

    def weighted(n, p):
        vT = vT_ref[n]
        return jnp.concatenate(
            [_dot(jnp.concatenate([vT[h * HEAD_DIM:(h + 1) * HEAD_DIM], ones], axis=0), p[:, h * blk:(h + 1) * blk])
             for h in range(H_MOBA)], axis=1)

    def pair(t, carry):
        m, acc = carry
        n0 = j - 2 * t
        n1 = jnp.maximum(n0 - 1, 0)
        s0 = scores(n0, jnp.where(t == 0, nblk, n0))
        s1 = scores(n1, jnp.where(n0 == 0, nblk + 1, n1))
        for n, s in ((n0, s0), (n1, s1)):
            m_new = jnp.maximum(m, jnp.max(s, axis=0, keepdims=True))
            p = jnp.exp2(s - m_new).astype(BF16)
            acc = jnp.exp2(m - m_new) * acc + weighted(n, p)
            m = m_new
        return m, acc

    init = (jnp.full((1, wide), NEG, F32), jnp.zeros((HEAD_DIM + ONES_ROWS, wide), F32))
    _, acc = lax.fori_loop(0, j // 2 + 1, pair, init)
    oT = acc[:HEAD_DIM] / acc[HEAD_DIM:HEAD_DIM + 1]
    o_ref[...] = jnp.concatenate([oT[:, h * blk:(h + 1) * blk] for h in range(H_MOBA)], axis=0).T


def _moba(qcT, kc, vcT, kmean, bias, batch, seq):
    t = kc.shape[0]
    nblk = seq // MOBA_BLOCK
    nfar = bias.shape[0] - 1
    return pl.pallas_call(
        functools.partial(_moba_kernel, nfar=nfar),
        grid=(batch, nblk),
        in_specs=[pl.BlockSpec((D_MOBA, MOBA_BLOCK), lambda b, j: (0, b * nblk + j)),
                  pl.BlockSpec((seq, D_MOBA), lambda b, j: (b, 0)),
                  pl.BlockSpec((nblk, D_MOBA, MOBA_BLOCK), lambda b, j: (b, 0, 0)),
                  pl.BlockSpec((nblk, 1, D_MOBA), lambda b, j: (b, 0, 0)),
                  pl.BlockSpec(bias.shape, lambda b, j: (0, 0, 0), pipeline_mode=pl.Buffered(1))],
        out_specs=pl.BlockSpec((MOBA_BLOCK, D_MOBA), lambda b, j: (b * nblk + j, 0)),
        out_shape=jax.ShapeDtypeStruct((t, D_MOBA), F32),
        scratch_shapes=[pltpu.VMEM((nblk + SUBLANES, H_MOBA * MOBA_BLOCK), F32)],
        compiler_params=pltpu.CompilerParams(dimension_semantics=("parallel", "arbitrary"),
                                             vmem_limit_bytes=VMEM_LIMIT),
        name="moba",
    )(qcT, kc, vcT, kmean, bias)


def _post_kernel(x_ref, oa_ref, ob_ref, oc_ref, gmix_ref, wo_ref, gmlp_ref, wup_ref, wdn_ref, gfin_ref,
                 y_ref, *, final, ff_chunk):
    gm = gmix_ref[...]
    x = x_ref[...]
    lo = 0
    for o_ref in (oa_ref, ob_ref, oc_ref):
        n = o_ref.shape[1]
        mixed = _rms(o_ref[...], gm[:, lo:lo + n]).astype(BF16)
        x = x + _dot(mixed, wo_ref[lo:lo + n, :])
        lo += n
    hb = _rms(x, gmlp_ref[...]).astype(BF16)
    y_ref[...] = x
    for c in range(wup_ref.shape[1] // ff_chunk):
        cs = slice(c * ff_chunk, (c + 1) * ff_chunk)
        u = jnp.maximum(_dot(hb, wup_ref[:, cs]), 0.0)
        y_ref[...] += _dot((u * u).astype(BF16), wdn_ref[cs, :])
    if final:
        y_ref[...] = _rms(y_ref[...], gfin_ref[...])


def _post(x, oa, ob, oc, layer, gmix, wo, gmlp, wup, wdn, gfin, final, tm=512, ff_chunk=1024):
    t, d = x.shape
    row = lambda i: (i, 0)
    stacked = lambda a: pl.BlockSpec((None,) + a.shape[1:], lambda i: (layer, 0, 0), pipeline_mode=pl.Buffered(1))
    return pl.pallas_call(
        functools.partial(_post_kernel, final=final, ff_chunk=ff_chunk),
        grid=(t // tm,),
        in_specs=[pl.BlockSpec((tm, d), row), pl.BlockSpec((tm, oa.shape[1]), row),
                  pl.BlockSpec((tm, ob.shape[1]), row), pl.BlockSpec((tm, oc.shape[1]), row),
                  stacked(gmix), stacked(wo), stacked(gmlp), stacked(wup), stacked(wdn),
                  pl.BlockSpec(gfin.shape, lambda i: (0, 0))],
        out_specs=pl.BlockSpec((tm, d), row),
        out_shape=jax.ShapeDtypeStruct((t, d), F32),
        compiler_params=pltpu.CompilerParams(dimension_semantics=("parallel",),
                                             vmem_limit_bytes=VMEM_LIMIT),
        name="post",
    )(x, oa, ob, oc, gmix, wo, gmlp, wup, wdn, gfin)


def _rope_tables(seq):
    inv_freq = ROPE_THETA ** (-jnp.arange(0, MLA_ROPE, 2, dtype=F32) / MLA_ROPE)
    ang = jnp.arange(seq, dtype=F32)[:, None] * inv_freq[None, :]
    cos, sin = jnp.cos(ang), jnp.sin(ang)
    half = MLA_ROPE // 2
    one = jnp.ones((seq, MLA_NOPE), F32)
    zero = jnp.zeros((seq, MLA_NOPE), F32)
    zh = jnp.zeros((seq, half), F32)
    tail1 = jnp.ones((seq, LANES - MLA_NOPE - MLA_ROPE), F32)
    tail0 = jnp.zeros((seq, LANES - MLA_NOPE - MLA_ROPE), F32)
    cos_t = jnp.concatenate([one, cos, cos, tail1], axis=1)
    sin_a = jnp.concatenate([zero, -sin, zh, tail0], axis=1)
    sin_b = jnp.concatenate([zero, zh, sin, tail0], axis=1)
    return cos_t, sin_a, sin_b, cos.T, sin.T


def _lookup(tab, bucket):
    col = lambda b: tab[b][(slice(None),) + (None,) * bucket.ndim]
    out = jnp.broadcast_to(col(0), (tab.shape[1],) + bucket.shape)
    for b in range(1, NUM_BUCKETS):
        out = jnp.where(bucket[None] == b, col(b), out)
    return out.astype(F32)


def _toeplitz(v, rows, cols):
    lead = v.shape[:-1]
    period = rows + cols
    u = jnp.pad(v[..., ::-1], [(0, 0)] * len(lead) + [(0, 1)])
    flat = jnp.tile(u, (1,) * len(lead) + (rows,))[..., :rows * (period - 1)]
    return flat.reshape(lead + (rows, period - 1))[..., rows - 1:rows - 1 + cols]


def _dil_bias(bias_tab):
    span = DIL_SPAN
    npat = len(DILATED_PATTERNS)
    diff = jnp.arange(3 * span - 1) - (span - 1)
    in_band = (diff >= 0) & (diff <= span)
    tabs = []
    for _, dil in DILATED_PATTERNS:
        by_diff = jnp.where(in_band[None], _lookup(bias_tab, _bucket(diff * dil)) * LOG2E, NEG)
        tabs.append(_toeplitz(by_diff, span, 2 * span))
    b = jnp.stack(tabs, axis=1)
    b = b.reshape(H_DIL // 2, 2, npat, span, 2 * span).transpose(0, 2, 1, 3, 4)
    return b.reshape(H_DIL // 2, npat, 2 * span, 2 * span)


def _moba_bias(bias_tab, nblk):
    blk = MOBA_BLOCK
    nfar = min(nblk - 1, REL_MAX_DISTANCE // blk + 1)
    dist = jnp.arange((nfar + 2) * blk - 1) - (blk - 1)
    by_dist = jnp.where(dist[None] >= 0, _lookup(bias_tab, _bucket(dist)) * LOG2E, NEG)
    windows = jnp.stack([by_dist[:, d * blk:(d + 2) * blk - 1] for d in range(nfar + 1)], axis=1)
    b = _toeplitz(windows, blk, blk)
    return b.transpose(1, 3, 0, 2).reshape(nfar + 1, blk, H_MOBA * blk)


def _pad_w_in(w):
    sizes = (D_DIL, D_DIL, D_DIL, MLA_Q_LORA, MLA_KV_LORA, MLA_ROPE, D_MOBA, D_MOBA, D_MOBA)
    parts, lo = [], 0
    for n in sizes:
        parts.append(w[..., lo:lo + n])
        lo += n
    zeros = lambda n: jnp.zeros(w.shape[:-1] + (n,), w.dtype)
    parts[5] = jnp.concatenate([zeros(MLA_NOPE), parts[5], zeros(LANES - MLA_NOPE - MLA_ROPE)], axis=-1)
    return jnp.concatenate(parts, axis=-1).astype(BF16)


def _pad_w_uq_t(w):
    depth, r, _ = w.shape
    w = w.reshape(depth, r, H_MLA, MLA_NOPE + MLA_ROPE)
    w = jnp.pad(w, ((0, 0), (0, 0), (0, 0), (0, LANES - MLA_NOPE - MLA_ROPE)))
    return w.reshape(depth, r, H_MLA * LANES).transpose(0, 2, 1).astype(BF16)


def _split_w_ukv(w):
    depth, r, _ = w.shape
    w = w.reshape(depth, r, H_MLA, MLA_NOPE + MLA_V)
    wk = jnp.pad(w[..., :MLA_NOPE], ((0, 0), (0, 0), (0, 0), (0, LANES - MLA_NOPE)))
    wv = w[..., MLA_NOPE:].reshape(depth, r, H_MLA * MLA_V)
    return wk.reshape(depth, r, H_MLA * LANES).astype(BF16), wv.transpose(0, 2, 1).astype(BF16)


def kernel(x, g_attn, w_in, g_q_lora, g_kv_lora, w_uq, w_ukv, rel_bias, g_mix, w_o, g_mlp, w_up, w_down,
           g_final):
    batch, seq, d = x.shape
    depth = w_in.shape[0]
    assert seq % (DIL_SPAN * max(dil for _, dil in DILATED_PATTERNS)) == 0
    assert seq % IN_TILE == 0 and IN_TILE % MLA_TK == 0
    cos_t, sin_a, sin_b, cosT, sinT = _rope_tables(seq)
    bias_dil = _dil_bias(rel_bias[:, :H_DIL])
    bias_moba = _moba_bias(rel_bias[:, H_DIL:], seq // MOBA_BLOCK)
    rows = lambda v: v[:, None, :].astype(F32)
    w = _pad_w_in(w_in)
    wk, wvT = _split_w_ukv(w_ukv)
    in_params = (rows(g_attn), w, rows(g_q_lora), rows(g_kv_lora), _pad_w_uq_t(w_uq), wk, wvT,
                 w[:, :, C_QC:C_QC + D_MOBA].transpose(0, 2, 1), w[:, :, C_VC:C_VC + D_MOBA].transpose(0, 2, 1))
    post_params = (rows(g_mix), w_o.astype(BF16), rows(g_mlp), w_up.astype(BF16), w_down.astype(BF16))
    gfin = g_final.reshape(1, -1).astype(F32)
    xf = x.reshape(batch * seq, d)
    for l in range(depth):
        qa, ka, va, qmT, km, vmT, qcT, kc, vcT, kmean = _in_proj(
            xf, l, *in_params, cos_t, sin_a, sin_b, cosT, sinT, seq, tm=IN_TILE)
        oa = _dilated(qa, ka, va, bias_dil, batch, seq)
        ob = _mla(qmT, km, vmT, batch, seq, tq=MLA_TQ, tk=MLA_TK)
        oc = _moba(qcT, kc, vcT, kmean, bias_moba, batch, seq)
        xf = _post(xf, oa, ob, oc, l, *post_params, gfin, final=(l == depth - 1))
    return xf.reshape(batch, seq, d)
```

```python
import functools
import math

import jax
import jax.numpy as jnp
from jax import lax
from jax.experimental import pallas as pl
from jax.experimental.pallas import tpu as pltpu

LANES = 128
SUBLANES = 8
HEAD_DIM = 64
H_DIL, H_MLA, H_MOBA = 6, 6, 4
D_DIL = H_DIL * HEAD_DIM
MLA_NOPE, MLA_ROPE, MLA_V = 64, 32, 64
MLA_Q_LORA, MLA_KV_LORA = 384, 128
D_MLA = H_MLA * MLA_V
D_MOBA = H_MOBA * HEAD_DIM
DILATED_PATTERNS = ((128, 1), (512, 4), (2048, 16))
DIL_SPAN = 128
MOBA_BLOCK = 256
MOBA_TOPK = 3
NUM_BUCKETS = 32
MAX_EXACT = 16
REL_MAX_DISTANCE = 2048
ROPE_THETA = 10000.0
EPS = 1e-6
NEG = -1e30
LOG2E = math.log2(math.e)
ONES_ROWS = 16
VMEM_LIMIT = 56 * 1024 * 1024
IN_TILE = 1024
MLA_TQ = 512
MLA_TK = 512

C_QA, C_KA, C_VA = 0, 384, 768
C_CQ, C_CKV, C_KR = 1152, 1536, 1664
C_QC, C_KC, C_VC = 1792, 2048, 2304
D_IN_PAD = 2560

BF16 = jnp.bfloat16
F32 = jnp.float32


def _bucket(dist):
    n = jnp.maximum(dist, 0)
    nf = jnp.maximum(n, 1).astype(F32)
    large = MAX_EXACT + (jnp.log(nf / MAX_EXACT) / math.log(REL_MAX_DISTANCE / MAX_EXACT)
                         * (NUM_BUCKETS - MAX_EXACT)).astype(jnp.int32)
    large = jnp.minimum(large, NUM_BUCKETS - 1)
    return jnp.where(n < MAX_EXACT, n, large)


def _rms(x, g):
    return x * lax.rsqrt(jnp.mean(x * x, axis=-1, keepdims=True) + EPS) * g


def _dot(a, b):
    return jnp.dot(a, b, preferred_element_type=F32)


def _dot_nt(a, b, precision=None):
    return lax.dot_general(a, b, (((1,), (1,)), ((), ())), preferred_element_type=F32,
                           precision=precision)


def _rope(x, cos, sin_a, sin_b):
    return x * cos + pltpu.roll(x, LANES - 16, 1) * sin_a + pltpu.roll(x, 16, 1) * sin_b


def _in_kernel(x_ref, g_ref, w_ref, gq_ref, gkv_ref, wuqT_ref, wk_ref, wvT_ref, wqcT_ref, wvcT_ref,
               cos_ref, sa_ref, sb_ref, cosT_ref, sinT_ref,
               qa_ref, ka_ref, va_ref, qmT_ref, km_ref, vmT_ref, qcT_ref, kc_ref, vcT_ref, kmean_ref):
    hb = _rms(x_ref[...], g_ref[...]).astype(BF16)

    def proj(c0, n):
        return _dot(hb, w_ref[:, c0:c0 + n])

    qa_ref[...] = proj(C_QA, D_DIL) * (HEAD_DIM ** -0.5 * LOG2E)
    ka_ref[...] = proj(C_KA, D_DIL)
    va_ref[...] = proj(C_VA, D_DIL)

    cq = _rms(proj(C_CQ, MLA_Q_LORA), gq_ref[...]).astype(BF16)
    qT = _dot_nt(wuqT_ref[...], cq)
    cosT, sinT = cosT_ref[...], sinT_ref[...]
    scale = (MLA_NOPE + MLA_ROPE) ** -0.5 * LOG2E
    half = MLA_ROPE // 2
    for h in range(H_MLA):
        r0 = h * LANES
        x1 = qT[r0 + MLA_NOPE:r0 + MLA_NOPE + half]
        x2 = qT[r0 + MLA_NOPE + half:r0 + MLA_NOPE + MLA_ROPE]
        roped = jnp.concatenate([qT[r0:r0 + MLA_NOPE], x1 * cosT - x2 * sinT, x2 * cosT + x1 * sinT,
                                 qT[r0 + MLA_NOPE + MLA_ROPE:r0 + LANES]], axis=0)
        qmT_ref[r0:r0 + LANES, :] = (roped * scale).astype(BF16)
    ckv = _rms(proj(C_CKV, MLA_KV_LORA), gkv_ref[...]).astype(BF16)
    kn = _dot(ckv, wk_ref[...])
    vmT = _dot_nt(wvT_ref[...], ckv).astype(BF16)
    for i in range(vmT_ref.shape[0]):
        vmT_ref[i] = vmT[:, i * MLA_TK:(i + 1) * MLA_TK]
    kr = _rope(proj(C_KR, LANES), cos_ref[...], sa_ref[...], sb_ref[...])
    for h in range(H_MLA):
        sl = slice(h * LANES, (h + 1) * LANES)
        km_ref[:, sl] = (kn[:, sl] + kr).astype(BF16)

    qcT_ref[...] = _dot_nt(wqcT_ref[...], hb) * (HEAD_DIM ** -0.5 * LOG2E)
    kc = proj(C_KC, D_MOBA)
    kc_ref[...] = kc.astype(BF16)
    vcT = _dot_nt(wvcT_ref[...], hb).astype(BF16)
    for i in range(kc.shape[0] // MOBA_BLOCK):
        sl = slice(i * MOBA_BLOCK, (i + 1) * MOBA_BLOCK)
        vcT_ref[i] = vcT[:, sl]
        kmean_ref[i] = jnp.mean(kc[sl], axis=0, keepdims=True)


def _in_proj(x, layer, g, w, gq, gkv, wuqT, wk, wvT, wqcT, wvcT, cos, sin_a, sin_b, cosT, sinT, seq, tm):
    t, d = x.shape
    nt = seq // tm
    row = lambda i: (i, 0)
    col = lambda i: (0, i)
    lead = lambda i: (i, 0, 0)
    pos = lambda i: (i % nt, 0)
    posT = lambda i: (0, i % nt)
    full = lambda a: pl.BlockSpec((None,) + a.shape[1:], lambda i: (layer, 0, 0), pipeline_mode=pl.Buffered(1))
    nb = tm // MOBA_BLOCK
    outs = [
        ((t, D_DIL), F32, (tm, D_DIL), row), ((t, D_DIL), F32, (tm, D_DIL), row),
        ((t, D_DIL), F32, (tm, D_DIL), row),
        ((H_MLA * LANES, t), BF16, (H_MLA * LANES, tm), col),
        ((t, H_MLA * LANES), BF16, (tm, H_MLA * LANES), row),
        ((t // MLA_TK, D_MLA, MLA_TK), BF16, (tm // MLA_TK, D_MLA, MLA_TK), lead),
        ((D_MOBA, t), F32, (D_MOBA, tm), col),
        ((t, D_MOBA), BF16, (tm, D_MOBA), row),
        ((t // MOBA_BLOCK, D_MOBA, MOBA_BLOCK), BF16, (nb, D_MOBA, MOBA_BLOCK), lead),
        ((t // MOBA_BLOCK, 1, D_MOBA), F32, (nb, 1, D_MOBA), lead),
    ]
    return pl.pallas_call(
        _in_kernel,
        grid=(t // tm,),
        in_specs=[pl.BlockSpec((tm, d), row), full(g), full(w), full(gq), full(gkv), full(wuqT), full(wk),
                  full(wvT), full(wqcT), full(wvcT),
                  pl.BlockSpec((tm, LANES), pos), pl.BlockSpec((tm, LANES), pos), pl.BlockSpec((tm, LANES), pos),
                  pl.BlockSpec((MLA_ROPE // 2, tm), posT), pl.BlockSpec((MLA_ROPE // 2, tm), posT)],
        out_specs=[pl.BlockSpec(blk, imap) for _, _, blk, imap in outs],
        out_shape=[jax.ShapeDtypeStruct(shape, dt) for shape, dt, _, _ in outs],
        compiler_params=pltpu.CompilerParams(dimension_semantics=("parallel",),
                                             vmem_limit_bytes=VMEM_LIMIT),
        name="in_proj",
    )(x, g, w, gq, gkv, wuqT, wk, wvT, wqcT, wvcT, cos, sin_a, sin_b, cosT, sinT)


def _dil_kernel(q_ref, k_ref, v_ref, bias_ref, o_ref, oscr, lscr, *, unroll):
    seq = q_ref.shape[0]
    span = DIL_SPAN
    lane = lax.broadcasted_iota(jnp.int32, (span, LANES), 1)
    head0 = lane < HEAD_DIM

    for pi, (_, dil) in enumerate(DILATED_PATTERNS):
        nblk = seq // (span * dil)

        def rows(start, dil=dil):
            return pl.ds(start, span) if dil == 1 else pl.ds(start, span, stride=dil)

        def unit(u, carry, first, pi=pi, dil=dil, rows=rows):
            r = u if first else u % dil
            cur = r if first else (u // dil) * (span * dil) + r
            q = q_ref[rows(cur), :]
            qs = jnp.concatenate([jnp.where(head0, q, 0.0), jnp.where(head0, 0.0, q)], axis=0).astype(BF16)
            if first:
                kk = k_ref[rows(cur), :].astype(BF16)
                vv = v_ref[rows(cur), :].astype(BF16)
                s = _dot_nt(qs, kk) + bias_ref[pi, :, span:]
            else:
                prev = cur - span * dil
                kk = jnp.concatenate([k_ref[rows(prev), :], k_ref[rows(cur), :]], axis=0).astype(BF16)
                vv = jnp.concatenate([v_ref[rows(prev), :], v_ref[rows(cur), :]], axis=0).astype(BF16)
                s = _dot_nt(qs, kk) + bias_ref[pi]
            m = jnp.max(s, axis=-1, keepdims=True)
            p = jnp.exp2(s - m).astype(BF16)
            on = _dot(p, jnp.concatenate([vv, jnp.ones_like(vv)], axis=1))
            l = on[:, LANES:]
            o = on[:, :LANES] / l
            lse = m + jnp.log(l) * LOG2E
            oscr[pi, rows(cur), :] = jnp.where(head0, o[:span], o[span:])
            lscr[pi, rows(cur), :] = jnp.where(head0, lse[:span], lse[span:])
            return carry

        lax.fori_loop(0, dil, functools.partial(unit, first=True), 0, unroll=min(unroll, dil))
        lax.fori_loop(dil, nblk * dil, functools.partial(unit, first=False), 0, unroll=unroll)

    ct = 512
    for c in range(seq // ct):
        sl = pl.ds(c * ct, ct)
        l0, l1, l2 = lscr[0, sl, :], lscr[1, sl, :], lscr[2, sl, :]
        mx = jnp.maximum(jnp.maximum(l0, l1), l2)
        e0, e1, e2 = jnp.exp2(l0 - mx), jnp.exp2(l1 - mx), jnp.exp2(l2 - mx)
        num = e0 * oscr[0, sl, :] + e1 * oscr[1, sl, :] + e2 * oscr[2, sl, :]
        o_ref[sl, :] = num / (e0 + e1 + e2)


def _dilated(qa, ka, va, bias, batch, seq, unroll=8):
    t = qa.shape[0]
    npair = H_DIL // 2
    blk = pl.BlockSpec((seq, LANES), lambda b, p: (b, p))
    return pl.pallas_call(
        functools.partial(_dil_kernel, unroll=unroll),
        grid=(batch, npair),
        in_specs=[blk, blk, blk,
                  pl.BlockSpec((None, len(DILATED_PATTERNS), 2 * DIL_SPAN, 2 * DIL_SPAN),
                               lambda b, p: (p, 0, 0, 0))],
        out_specs=blk,
        out_shape=jax.ShapeDtypeStruct((t, D_DIL), F32),
        scratch_shapes=[pltpu.VMEM((len(DILATED_PATTERNS), seq, LANES), F32),
                        pltpu.VMEM((len(DILATED_PATTERNS), seq, LANES), F32)],
        compiler_params=pltpu.CompilerParams(dimension_semantics=("parallel", "parallel"),
                                             vmem_limit_bytes=VMEM_LIMIT),
        name="dilated",
    )(qa, ka, va, bias)


def _mla_kernel(qT_ref, k_ref, vT_ref, o_ref, *, tq, tk, nsub):
    i = pl.program_id(2)
    sub = tk // nsub
    zero = jnp.zeros((LANES, tq), BF16)
    rhs = jnp.concatenate([jnp.concatenate([qT_ref[:LANES, :], zero], axis=1),
                           jnp.concatenate([zero, qT_ref[LANES:, :]], axis=1)], axis=0)
    ones = jnp.ones((ONES_ROWS, sub), BF16)

    def both(x, lo):
        return x if lo == 0 else jnp.concatenate([x[:, lo:tq], x[:, tq + lo:]], axis=1)

    def scores(j, u, lo, diag):
        w = tq - lo
        s = _dot(k_ref[pl.ds(pl.multiple_of(j * tk + u * sub, sub), sub), :], both(rhs, lo))
        if diag:
            krow = lax.broadcasted_iota(jnp.int32, (sub, 2 * w), 0) + u * sub
            qcol = lax.broadcasted_iota(jnp.int32, (sub, 2 * w), 1)
            qcol = jnp.where(qcol >= w, qcol - w, qcol) + lo
            s = jnp.where(krow <= qcol, s, NEG)
        return s

    def update(j, u, s, m, acc):
        w = s.shape[1] // 2
        m_new = jnp.maximum(m, jnp.max(s, axis=0, keepdims=True))
        p = jnp.exp2(s - m_new).astype(BF16)
        vu = vT_ref[j][:, u * sub:(u + 1) * sub]
        pv = jnp.concatenate([_dot(jnp.concatenate([vu[:MLA_V], ones], axis=0), p[:, :w]),
                              _dot(jnp.concatenate([vu[MLA_V:], ones], axis=0), p[:, w:])], axis=1)
        return m_new, jnp.exp2(m - m_new) * acc + pv

    def past(tiles, carry):
        ss = [(j, u, scores(j, u, 0, False)) for j in tiles for u in range(nsub)]
        for j, u, s in ss:
            carry = update(j, u, s, *carry)
        return carry

    init = (jnp.full((1, 2 * tq), NEG, F32), jnp.zeros((MLA_V + ONES_ROWS, 2 * tq), F32))
    carry = lax.fori_loop(0, i // 2, lambda t, c: past((2 * t, 2 * t + 1), c), init)
    m, acc = lax.fori_loop(2 * (i // 2), i, lambda j, c: past((j,), c), carry)
    ss = [scores(i, u, u * sub, True) for u in range(nsub)]
    for u, s in enumerate(ss):
        lo = u * sub
        m_u, acc_u = update(i, u, s, both(m, lo), both(acc, lo))
        if lo:
            m = jnp.concatenate([m[:, :lo], m_u[:, :tq - lo], m[:, tq:tq + lo], m_u[:, tq - lo:]], axis=1)
            acc = jnp.concatenate([acc[:, :lo], acc_u[:, :tq - lo], acc[:, tq:tq + lo], acc_u[:, tq - lo:]], axis=1)
        else:
            m, acc = m_u, acc_u
    oT = acc[:MLA_V] / acc[MLA_V:MLA_V + 1]
    o_ref[...] = jnp.concatenate([oT[:, :tq], oT[:, tq:]], axis=0).T


def _mla(qmT, km, vmT, batch, seq, tq, tk, nsub=2):
    t = km.shape[0]
    nq = seq // tq
    npair = H_MLA // 2
    assert tk == tq and seq % tk == 0
    return pl.pallas_call(
        functools.partial(_mla_kernel, tq=tq, tk=tk, nsub=nsub),
        grid=(batch, npair, nq),
        in_specs=[pl.BlockSpec((2 * LANES, tq), lambda b, p, i: (p, b * nq + i)),
                  pl.BlockSpec((seq, 2 * LANES), lambda b, p, i: (b, p)),
                  pl.BlockSpec((seq // tk, LANES, tk), lambda b, p, i: (b, p, 0))],
        out_specs=pl.BlockSpec((tq, LANES), lambda b, p, i: (b * nq + i, p)),
        out_shape=jax.ShapeDtypeStruct((t, D_MLA), F32),
        compiler_params=pltpu.CompilerParams(dimension_semantics=("parallel", "parallel", "arbitrary"),
                                             vmem_limit_bytes=VMEM_LIMIT),
        name="mla",
    )(qmT, km, vmT)


def _moba_kernel(qT_ref, k_ref, vT_ref, kmean_ref, bias_ref, o_ref, pen_ref, *, nfar):
    blk = MOBA_BLOCK
    wide = H_MOBA * blk
    j = pl.program_id(1)
    nblk = kmean_ref.shape[0]
    qT = qT_ref[...]
    rhead = lax.broadcasted_iota(jnp.int32, (D_MOBA, blk), 0) // HEAD_DIM
    qsT = jnp.concatenate([jnp.where(rhead == h, qT, 0.0) for h in range(H_MOBA)], axis=1)

    gate = jnp.dot(kmean_ref[:, 0, :], qsT, preferred_element_type=F32, precision=lax.Precision.HIGHEST)
    bidx = lax.broadcasted_iota(jnp.int32, (nblk, wide), 0)
    gate = jnp.where(bidx < j, gate, -jnp.inf)
    pen = jnp.full((nblk, wide), NEG, F32)
    for _ in range(MOBA_TOPK):
        mx = jnp.max(gate, axis=0, keepdims=True)
        first = jnp.min(jnp.where(gate == mx, bidx, nblk), axis=0, keepdims=True)
        hit = bidx == first
        pen = jnp.where(hit & (mx > -jnp.inf), 0.0, pen)
        gate = jnp.where(hit, -jnp.inf, gate)

    pen_ref[0:nblk, :] = pen
    pen_ref[nblk:nblk + 1, :] = jnp.zeros((1, wide), F32)
    pen_ref[nblk + 1:nblk + 2, :] = jnp.full((1, wide), NEG, F32)
    rhs = qsT.astype(BF16)

    def scores(n, pick):
        s = _dot(k_ref[pl.ds(pl.multiple_of(n * blk, blk), blk), :], rhs)
        return s + bias_ref[jnp.minimum(j - n, nfar)] + pen_ref[pl.ds(pick, 1), :]

    ones = jnp.ones((ONES_ROWS, blk), BF16)

    def weighted(n, p):
        vT = vT_ref[n]
        return jnp.concatenate(
            [_dot(jnp.concatenate([vT[h * HEAD_DIM:(h + 1) * HEAD_DIM], ones], axis=0), p[:, h * blk:(h + 1) * blk])
             for h in range(H_MOBA)], axis=1)

    def pair(t, carry):
        m, acc = carry
        n0 = j - 2 * t
        n1 = jnp.maximum(n0 - 1, 0)
        s0 = scores(n0, jnp.where(t == 0, nblk, n0))
        s1 = scores(n1, jnp.where(n0 == 0, nblk + 1, n1))
        for n, s in ((n0, s0), (n1, s1)):
            m_new = jnp.maximum(m, jnp.max(s, axis=0, keepdims=True))
            p = jnp.exp2(s - m_new).astype(BF16)
            acc = jnp.exp2(m - m_new) * acc + weighted(n, p)
            m = m_new
        return m, acc

    init = (jnp.full((1, wide), NEG, F32), jnp.zeros((HEAD_DIM + ONES_ROWS, wide), F32))
    _, acc = lax.fori_loop(0, j // 2 + 1, pair, init)
    oT = acc[:HEAD_DIM] / acc[HEAD_DIM:HEAD_DIM + 1]
    o_ref[...] = jnp.concatenate([oT[:, h * blk:(h + 1) * blk] for h in range(H_MOBA)], axis=0).T


def _moba(qcT, kc, vcT, kmean, bias, batch, seq):
    t = kc.shape[0]
    nblk = seq // MOBA_BLOCK
    nfar = bias.shape[0] - 1
    return pl.pallas_call(
        functools.partial(_moba_kernel, nfar=nfar),
        grid=(batch, nblk),
        in_specs=[pl.BlockSpec((D_MOBA, MOBA_BLOCK), lambda b, j: (0, b * nblk + j)),
                  pl.BlockSpec((seq, D_MOBA), lambda b, j: (b, 0)),
                  pl.BlockSpec((nblk, D_MOBA, MOBA_BLOCK), lambda b, j: (b, 0, 0)),
                  pl.BlockSpec((nblk, 1, D_MOBA), lambda b, j: (b, 0, 0)),
                  pl.BlockSpec(bias.shape, lambda b, j: (0, 0, 0), pipeline_mode=pl.Buffered(1))],
        out_specs=pl.BlockSpec((MOBA_BLOCK, D_MOBA), lambda b, j: (b * nblk + j, 0)),
        out_shape=jax.ShapeDtypeStruct((t, D_MOBA), F32),
        scratch_shapes=[pltpu.VMEM((nblk + SUBLANES, H_MOBA * MOBA_BLOCK), F32)],
        compiler_params=pltpu.CompilerParams(dimension_semantics=("parallel", "arbitrary"),
                                             vmem_limit_bytes=VMEM_LIMIT),
        name="moba",
    )(qcT, kc, vcT, kmean, bias)


def _post_kernel(x_ref, oa_ref, ob_ref, oc_ref, gmix_ref, wo_ref, gmlp_ref, wup_ref, wdn_ref, gfin_ref,
                 y_ref, *, final, ff_chunk):
    gm = gmix_ref[...]
    x = x_ref[...]
    lo = 0
    for o_ref in (oa_ref, ob_ref, oc_ref):
        n = o_ref.shape[1]
        mixed = _rms(o_ref[...], gm[:, lo:lo + n]).astype(BF16)
        x = x + _dot(mixed, wo_ref[lo:lo + n, :])
        lo += n
    hb = _rms(x, gmlp_ref[...]).astype(BF16)
    y_ref[...] = x
    for c in range(wup_ref.shape[1] // ff_chunk):
        cs = slice(c * ff_chunk, (c + 1) * ff_chunk)
        u = jnp.maximum(_dot(hb, wup_ref[:, cs]), 0.0)
        y_ref[...] += _dot((u * u).astype(BF16), wdn_ref[cs, :])
    if final:
        y_ref[...] = _rms(y_ref[...], gfin_ref[...])


def _post(x, oa, ob, oc, layer, gmix, wo, gmlp, wup, wdn, gfin, final, tm=512, ff_chunk=1024):
    t, d = x.shape
    row = lambda i: (i, 0)
    stacked = lambda a: pl.BlockSpec((None,) + a.shape[1:], lambda i: (layer, 0, 0), pipeline_mode=pl.Buffered(1))
    return pl.pallas_call(
        functools.partial(_post_kernel, final=final, ff_chunk=ff_chunk),
        grid=(t // tm,),
        in_specs=[pl.BlockSpec((tm, d), row), pl.BlockSpec((tm, oa.shape[1]), row),
                  pl.BlockSpec((tm, ob.shape[1]), row), pl.BlockSpec((tm, oc.shape[1]), row),
                  stacked(gmix), stacked(wo), stacked(gmlp), stacked(wup), stacked(wdn),
                  pl.BlockSpec(gfin.shape, lambda i: (0, 0))],
        out_specs=pl.BlockSpec((tm, d), row),
        out_shape=jax.ShapeDtypeStruct((t, d), F32),
        compiler_params=pltpu.CompilerParams(dimension_semantics=("parallel",),
                                             vmem_limit_bytes=VMEM_LIMIT),
        name="post",
    )(x, oa, ob, oc, gmix, wo, gmlp, wup, wdn, gfin)


def _rope_tables(seq):
    inv_freq = ROPE_THETA ** (-jnp.arange(0, MLA_ROPE, 2, dtype=F32) / MLA_ROPE)
    ang = jnp.arange(seq, dtype=F32)[:, None] * inv_freq[None, :]
    cos, sin = jnp.cos(ang), jnp.sin(ang)
    half = MLA_ROPE // 2
    one = jnp.ones((seq, MLA_NOPE), F32)
    zero = jnp.zeros((seq, MLA_NOPE), F32)
    zh = jnp.zeros((seq, half), F32)
    tail1 = jnp.ones((seq, LANES - MLA_NOPE - MLA_ROPE), F32)
    tail0 = jnp.zeros((seq, LANES - MLA_NOPE - MLA_ROPE), F32)
    cos_t = jnp.concatenate([one, cos, cos, tail1], axis=1)
    sin_a = jnp.concatenate([zero, -sin, zh, tail0], axis=1)
    sin_b = jnp.concatenate([zero, zh, sin, tail0], axis=1)
    return cos_t, sin_a, sin_b, cos.T, sin.T


def _lookup(tab, bucket):
    col = lambda b: tab[b][(slice(None),) + (None,) * bucket.ndim]
    out = jnp.broadcast_to(col(0), (tab.shape[1],) + bucket.shape)
    for b in range(1, NUM_BUCKETS):
        out = jnp.where(bucket[None] == b, col(b), out)
    return out.astype(F32)


def _toeplitz(v, rows, cols):
    lead = v.shape[:-1]
    period = rows + cols
    u = jnp.pad(v[..., ::-1], [(0, 0)] * len(lead) + [(0, 1)])
    flat = jnp.tile(u, (1,) * len(lead) + (rows,))[..., :rows * (period - 1)]
    return flat.reshape(lead + (rows, period - 1))[..., rows - 1:rows - 1 + cols]


def _dil_bias(bias_tab):
    span = DIL_SPAN
    npat = len(DILATED_PATTERNS)
    diff = jnp.arange(3 * span - 1) - (span - 1)
    in_band = (diff >= 0) & (diff <= span)
    tabs = []
    for _, dil in DILATED_PATTERNS:
        by_diff = jnp.where(in_band[None], _lookup(bias_tab, _bucket(diff * dil)) * LOG2E, NEG)
        tabs.append(_toeplitz(by_diff, span, 2 * span))
    b = jnp.stack(tabs, axis=1)
    b = b.reshape(H_DIL // 2, 2, npat, span, 2 * span).transpose(0, 2, 1, 3, 4)
    return b.reshape(H_DIL // 2, npat, 2 * span, 2 * span)


def _moba_bias(bias_tab, nblk):
    blk = MOBA_BLOCK
    nfar = min(nblk - 1, REL_MAX_DISTANCE // blk + 1)
    dist = jnp.arange((nfar + 2) * blk - 1) - (blk - 1)
    by_dist = jnp.where(dist[None] >= 0, _lookup(bias_tab, _bucket(dist)) * LOG2E, NEG)
    windows = jnp.stack([by_dist[:, d * blk:(d + 2) * blk - 1] for d in range(nfar + 1)], axis=1)
    b = _toeplitz(windows, blk, blk)
    return b.transpose(1, 3, 0, 2).reshape(nfar + 1, blk, H_MOBA * blk)


def _pad_w_in(w):
    sizes = (D_DIL, D_DIL, D_DIL, MLA_Q_LORA, MLA_KV_LORA, MLA_ROPE, D_MOBA, D_MOBA, D_MOBA)
    parts, lo = [], 0
    for n in sizes:
        parts.append(w[..., lo:lo + n])
        lo += n
    zeros = lambda n: jnp.zeros(w.shape[:-1] + (n,), w.dtype)
    parts[5] = jnp.concatenate([zeros(MLA_NOPE), parts[5], zeros(LANES - MLA_NOPE - MLA_ROPE)], axis=-1)
    return jnp.concatenate(parts, axis=-1).astype(BF16)


def _pad_w_uq_t(w):
    depth, r, _ = w.shape
    w = w.reshape(depth, r, H_MLA, MLA_NOPE + MLA_ROPE)
    w = jnp.pad(w, ((0, 0), (0, 0), (0, 0), (0, LANES - MLA_NOPE - MLA_ROPE)))
    return w.reshape(depth, r, H_MLA * LANES).transpose(0, 2, 1).astype(BF16)


def _split_w_ukv(w):
    depth, r, _ = w.shape
    w = w.reshape(depth, r, H_MLA, MLA_NOPE + MLA_V)
    wk = jnp.pad(w[..., :MLA_NOPE], ((0, 0), (0, 0), (0, 0), (0, LANES - MLA_NOPE)))
    wv = w[..., MLA_NOPE:].reshape(depth, r, H_MLA * MLA_V)
    return wk.reshape(depth, r, H_MLA * LANES).astype(BF16), wv.transpose(0, 2, 1).astype(BF16)


def kernel(x, g_attn, w_in, g_q_lora, g_kv_lora, w_uq, w_ukv, rel_bias, g_mix, w_o, g_mlp, w_up, w_down,
           g_final):
    batch, seq, d = x.shape
    depth = w_in.shape[0]
    assert seq % (DIL_SPAN * max(dil for _, dil in DILATED_PATTERNS)) == 0
    assert seq % IN_TILE == 0 and IN_TILE % MLA_TK == 0
    cos_t, sin_a, sin_b, cosT, sinT = _rope_tables(seq)
    bias_dil = _dil_bias(rel_bias[:, :H_DIL])
    bias_moba = _moba_bias(rel_bias[:, H_DIL:], seq // MOBA_BLOCK)
    rows = lambda v: v[:, None, :].astype(F32)
    w = _pad_w_in(w_in)
    wk, wvT = _split_w_ukv(w_ukv)
    in_params = (rows(g_attn), w, rows(g_q_lora), rows(g_kv_lora), _pad_w_uq_t(w_uq), wk, wvT,
                 w[:, :, C_QC:C_QC + D_MOBA].transpose(0, 2, 1), w[:, :, C_VC:C_VC + D_MOBA].transpose(0, 2, 1))
    post_params = (rows(g_mix), w_o.astype(BF16), rows(g_mlp), w_up.astype(BF16), w_down.astype(BF16))
    gfin = g_final.reshape(1, -1).astype(F32)
    xf = x.reshape(batch * seq, d)
    for l in range(depth):
        qa, ka, va, qmT, km, vmT, qcT, kc, vcT, kmean = _in_proj(
            xf, l, *in_params, cos_t, sin_a, sin_b, cosT, sinT, seq, tm=IN_TILE)
        oa = _dilated(qa, ka, va, bias_dil, batch, seq)
        ob = _mla(qmT, km, vmT, batch, seq, tq=MLA_TQ, tk=MLA_TK)
        oc = _moba(qcT, kc, vcT, kmean, bias_moba, batch, seq)
        xf = _post(xf, oa, ob, oc, l, *post_params, gfin, final=(l == depth - 1))
    return xf.reshape(batch, seq, d)
```

```python
import functools
import math

import jax
import jax.numpy as jnp
from jax import lax
from jax.experimental import pallas as pl
from jax.experimental.pallas import tpu as pltpu

LANES = 128
SUBLANES = 8
HEAD_DIM = 64
H_DIL, H_MLA, H_MOBA = 6, 6, 4
D_DIL = H_DIL * HEAD_DIM
MLA_NOPE, MLA_ROPE, MLA_V = 64, 32, 64
MLA_Q_LORA, MLA_KV_LORA = 384, 128
D_MLA = H_MLA * MLA_V
D_MOBA = H_MOBA * HEAD_DIM
DILATED_PATTERNS = ((128, 1), (512, 4), (2048, 16))
DIL_SPAN = 128
MOBA_BLOCK = 256
MOBA_TOPK = 3
NUM_BUCKETS = 32
MAX_EXACT = 16
REL_MAX_DISTANCE = 2048
ROPE_THETA = 10000.0
EPS = 1e-6
NEG = -1e30
LOG2E = math.log2(math.e)
ONES_ROWS = 16
VMEM_LIMIT = 56 * 1024 * 1024
IN_TILE = 1024
MLA_TQ = 512
MLA_TK = 512

C_QA, C_KA, C_VA = 0, 384, 768
C_CQ, C_CKV, C_KR = 1152, 1536, 1664
C_QC, C_KC, C_VC = 1792, 2048, 2304
D_IN_PAD = 2560

BF16 = jnp.bfloat16
F32 = jnp.float32


def _bucket(dist):
    n = jnp.maximum(dist, 0)
    nf = jnp.maximum(n, 1).astype(F32)
    large = MAX_EXACT + (jnp.log(nf / MAX_EXACT) / math.log(REL_MAX_DISTANCE / MAX_EXACT)
                         * (NUM_BUCKETS - MAX_EXACT)).astype(jnp.int32)
    large = jnp.minimum(large, NUM_BUCKETS - 1)
    return jnp.where(n < MAX_EXACT, n, large)


def _rms(x, g):
    return x * lax.rsqrt(jnp.mean(x * x, axis=-1, keepdims=True) + EPS) * g


def _dot(a, b):
    return jnp.dot(a, b, preferred_element_type=F32)


def _dot_nt(a, b, precision=None):
    return lax.dot_general(a, b, (((1,), (1,)), ((), ())), preferred_element_type=F32,
                           precision=precision)


def _rope(x, cos, sin_a, sin_b):
    return x * cos + pltpu.roll(x, LANES - 16, 1) * sin_a + pltpu.roll(x, 16, 1) * sin_b


def _in_kernel(x_ref, g_ref, w_ref, gq_ref, gkv_ref, wuqT_ref, wk_ref, wvT_ref, wqcT_ref, wvcT_ref,
               cos_ref, sa_ref, sb_ref, cosT_ref, sinT_ref,
               qa_ref, ka_ref, va_ref, qmT_ref, km_ref, vmT_ref, qcT_ref, kc_ref, vcT_ref, kmean_ref):
    hb = _rms(x_ref[...], g_ref[...]).astype(BF16)

    def proj(c0, n):
        return _dot(hb, w_ref[:, c0:c0 + n])

    qa_ref[...] = proj(C_QA, D_DIL) * (HEAD_DIM ** -0.5 * LOG2E)
    ka_ref[...] = proj(C_KA, D_DIL)
    va_ref[...] = proj(C_VA, D_DIL)

    cq = _rms(proj(C_CQ, MLA_Q_LORA), gq_ref[...]).astype(BF16)
    qT = _dot_nt(wuqT_ref[...], cq)
    cosT, sinT = cosT_ref[...], sinT_ref[...]
    scale = (MLA_NOPE + MLA_ROPE) ** -0.5 * LOG2E
    half = MLA_ROPE // 2
    for h in range(H_MLA):
        r0 = h * LANES
        x1 = qT[r0 + MLA_NOPE:r0 + MLA_NOPE + half]
        x2 = qT[r0 + MLA_NOPE + half:r0 + MLA_NOPE + MLA_ROPE]
        roped = jnp.concatenate([qT[r0:r0 + MLA_NOPE], x1 * cosT - x2 * sinT, x2 * cosT + x1 * sinT,
                                 qT[r0 + MLA_NOPE + MLA_ROPE:r0 + LANES]], axis=0)
        qmT_ref[r0:r0 + LANES, :] = (roped * scale).astype(BF16)
    ckv = _rms(proj(C_CKV, MLA_KV_LORA), gkv_ref[...]).astype(BF16)
    kn = _dot(ckv, wk_ref[...])
    vmT = _dot_nt(wvT_ref[...], ckv).astype(BF16)
    for i in range(vmT_ref.shape[0]):
        vmT_ref[i] = vmT[:, i * MLA_TK:(i + 1) * MLA_TK]
    kr = _rope(proj(C_KR, LANES), cos_ref[...], sa_ref[...], sb_ref[...])
    for h in range(H_MLA):
        sl = slice(h * LANES, (h + 1) * LANES)
        km_ref[:, sl] = (kn[:, sl] + kr).astype(BF16)

    qcT_ref[...] = _dot_nt(wqcT_ref[...], hb) * (HEAD_DIM ** -0.5 * LOG2E)
    kc = proj(C_KC, D_MOBA)
    kc_ref[...] = kc.astype(BF16)
    vcT = _dot_nt(wvcT_ref[...], hb).astype(BF16)
    for i in range(kc.shape[0] // MOBA_BLOCK):
        sl = slice(i * MOBA_BLOCK, (i + 1) * MOBA_BLOCK)
        vcT_ref[i] = vcT[:, sl]
        kmean_ref[i] = jnp.mean(kc[sl], axis=0, keepdims=True)


def _in_proj(x, layer, g, w, gq, gkv, wuqT, wk, wvT, wqcT, wvcT, cos, sin_a, sin_b, cosT, sinT, seq, tm):
    t, d = x.shape
    nt = seq // tm
    row = lambda i: (i, 0)
    col = lambda i: (0, i)
    lead = lambda i: (i, 0, 0)
    pos = lambda i: (i % nt, 0)
    posT = lambda i: (0, i % nt)
    full = lambda a: pl.BlockSpec((None,) + a.shape[1:], lambda i: (layer, 0, 0), pipeline_mode=pl.Buffered(1))
    nb = tm // MOBA_BLOCK
    outs = [
        ((t, D_DIL), F32, (tm, D_DIL), row), ((t, D_DIL), F32, (tm, D_DIL), row),
        ((t, D_DIL), F32, (tm, D_DIL), row),
        ((H_MLA * LANES, t), BF16, (H_MLA * LANES, tm), col),
        ((t, H_MLA * LANES), BF16, (tm, H_MLA * LANES), row),
        ((t // MLA_TK, D_MLA, MLA_TK), BF16, (tm // MLA_TK, D_MLA, MLA_TK), lead),
        ((D_MOBA, t), F32, (D_MOBA, tm), col),
        ((t, D_MOBA), BF16, (tm, D_MOBA), row),
        ((t // MOBA_BLOCK, D_MOBA, MOBA_BLOCK), BF16, (nb, D_MOBA, MOBA_BLOCK), lead),
        ((t // MOBA_BLOCK, 1, D_MOBA), F32, (nb, 1, D_MOBA), lead),
    ]
    return pl.pallas_call(
        _in_kernel,
        grid=(t // tm,),
        in_specs=[pl.BlockSpec((tm, d), row), full(g), full(w), full(gq), full(gkv), full(wuqT), full(wk),
                  full(wvT), full(wqcT), full(wvcT),
                  pl.BlockSpec((tm, LANES), pos), pl.BlockSpec((tm, LANES), pos), pl.BlockSpec((tm, LANES), pos),
                  pl.BlockSpec((MLA_ROPE // 2, tm), posT), pl.BlockSpec((MLA_ROPE // 2, tm), posT)],
        out_specs=[pl.BlockSpec(blk, imap) for _, _, blk, imap in outs],
        out_shape=[jax.ShapeDtypeStruct(shape, dt) for shape, dt, _, _ in outs],
        compiler_params=pltpu.CompilerParams(dimension_semantics=("parallel",),
                                             vmem_limit_bytes=VMEM_LIMIT),
        name="in_proj",
    )(x, g, w, gq, gkv, wuqT, wk, wvT, wqcT, wvcT, cos, sin_a, sin_b, cosT, sinT)


def _dil_kernel(q_ref, k_ref, v_ref, bias_ref, o_ref, oscr, lscr, *, unroll):
    seq = q_ref.shape[0]
    span = DIL_SPAN
    lane = lax.broadcasted_iota(jnp.int32, (span, LANES), 1)
    head0 = lane < HEAD_DIM

    for pi, (_, dil) in enumerate(DILATED_PATTERNS):
        nblk = seq // (span * dil)

        def rows(start, dil=dil):
            return pl.ds(start, span) if dil == 1 else pl.ds(start, span, stride=dil)

        def unit(u, carry, first, pi=pi, dil=dil, rows=rows):
            r = u if first else u % dil
            cur = r if first else (u // dil) * (span * dil) + r
            q = q_ref[rows(cur), :]
            qs = jnp.concatenate([jnp.where(head0, q, 0.0), jnp.where(head0, 0.0, q)], axis=0).astype(BF16)
            if first:
                kk = k_ref[rows(cur), :].astype(BF16)
                vv = v_ref[rows(cur), :].astype(BF16)
                s = _dot_nt(qs, kk) + bias_ref[pi, :, span:]
            else:
                prev = cur - span * dil
                kk = jnp.concatenate([k_ref[rows(prev), :], k_ref[rows(cur), :]], axis=0).astype(BF16)
                vv = jnp.concatenate([v_ref[rows(prev), :], v_ref[rows(cur), :]], axis=0).astype(BF16)
                s = _dot_nt(qs, kk) + bias_ref[pi]
            m = jnp.max(s, axis=-1, keepdims=True)
            p = jnp.exp2(s - m).astype(BF16)
            on = _dot(p, jnp.concatenate([vv, jnp.ones_like(vv)], axis=1))
            l = on[:, LANES:]
            o = on[:, :LANES] / l
            lse = m + jnp.log(l) * LOG2E
            oscr[pi, rows(cur), :] = jnp.where(head0, o[:span], o[span:])
            lscr[pi, rows(cur), :] = jnp.where(head0, lse[:span], lse[span:])
            return carry

        lax.fori_loop(0, dil, functools.partial(unit, first=True), 0, unroll=min(unroll, dil))
        lax.fori_loop(dil, nblk * dil, functools.partial(unit, first=False), 0, unroll=unroll)

    ct = 512
    for c in range(seq // ct):
        sl = pl.ds(c * ct, ct)
        l0, l1, l2 = lscr[0, sl, :], lscr[1, sl, :], lscr[2, sl, :]
        mx = jnp.maximum(jnp.maximum(l0, l1), l2)
        e0, e1, e2 = jnp.exp2(l0 - mx), jnp.exp2(l1 - mx), jnp.exp2(l2 - mx)
        num = e0 * oscr[0, sl, :] + e1 * oscr[1, sl, :] + e2 * oscr[2, sl, :]
        o_ref[sl, :] = num / (e0 + e1 + e2)


def _dilated(qa, ka, va, bias, batch, seq, unroll=8):
    t = qa.shape[0]
    npair = H_DIL // 2
    blk = pl.BlockSpec((seq, LANES), lambda b, p: (b, p))
    return pl.pallas_call(
        functools.partial(_dil_kernel, unroll=unroll),
        grid=(batch, npair),
        in_specs=[blk, blk, blk,
                  pl.BlockSpec((None, len(DILATED_PATTERNS), 2 * DIL_SPAN, 2 * DIL_SPAN),
                               lambda b, p: (p, 0, 0, 0))],
        out_specs=blk,
        out_shape=jax.ShapeDtypeStruct((t, D_DIL), F32),
        scratch_shapes=[pltpu.VMEM((len(DILATED_PATTERNS), seq, LANES), F32),
                        pltpu.VMEM((len(DILATED_PATTERNS), seq, LANES), F32)],
        compiler_params=pltpu.CompilerParams(dimension_semantics=("parallel", "parallel"),
                                             vmem_limit_bytes=VMEM_LIMIT),
        name="dilated",
    )(qa, ka, va, bias)


def _mla_kernel(qT_ref, k_ref, vT_ref, o_ref, *, tq, tk, nsub):
    i = pl.program_id(2)
    sub = tk // nsub
    zero = jnp.zeros((LANES, tq), BF16)
    rhs = jnp.concatenate([jnp.concatenate([qT_ref[:LANES, :], zero], axis=1),
                           jnp.concatenate([zero, qT_ref[LANES:, :]], axis=1)], axis=0)
    ones = jnp.ones((ONES_ROWS, sub), BF16)

    def both(x, lo):
        return x if lo == 0 else jnp.concatenate([x[:, lo:tq], x[:, tq + lo:]], axis=1)

    def scores(j, u, lo, diag):
        w = tq - lo
        s = _dot(k_ref[pl.ds(pl.multiple_of(j * tk + u * sub, sub), sub), :], both(rhs, lo))
        if diag:
            krow = lax.broadcasted_iota(jnp.int32, (sub, 2 * w), 0) + u * sub
            qcol = lax.broadcasted_iota(jnp.int32, (sub, 2 * w), 1)
            qcol = jnp.where(qcol >= w, qcol - w, qcol) + lo
            s = jnp.where(krow <= qcol, s, NEG)
        return s

    def update(j, u, s, m, acc):
        w = s.shape[1] // 2
        m_new = jnp.maximum(m, jnp.max(s, axis=0, keepdims=True))
        p = jnp.exp2(s - m_new).astype(BF16)
        vu = vT_ref[j][:, u * sub:(u + 1) * sub]
        pv = jnp.concatenate([_dot(jnp.concatenate([vu[:MLA_V], ones], axis=0), p[:, :w]),
                              _dot(jnp.concatenate([vu[MLA_V:], ones], axis=0), p[:, w:])], axis=1)
        return m_new, jnp.exp2(m - m_new) * acc + pv

    def past(tiles, carry):
        ss = [(j, u, scores(j, u, 0, False)) for j in tiles for u in range(nsub)]
        for j, u, s in ss:
            carry = update(j, u, s, *carry)
        return carry

    init = (jnp.full((1, 2 * tq), NEG, F32), jnp.zeros((MLA_V + ONES_ROWS, 2 * tq), F32))
    carry = lax.fori_loop(0, i // 2, lambda t, c: past((2 * t, 2 * t + 1), c), init)
    m, acc = lax.fori_loop(2 * (i // 2), i, lambda j, c: past((j,), c), carry)
    ss = [scores(i, u, u * sub, True) for u in range(nsub)]
    for u, s in enumerate(ss):
        lo = u * sub
        m_u, acc_u = update(i, u, s, both(m, lo), both(acc, lo))
        if lo:
            m = jnp.concatenate([m[:, :lo], m_u[:, :tq - lo], m[:, tq:tq + lo], m_u[:, tq - lo:]], axis=1)
            acc = jnp.concatenate([acc[:, :lo], acc_u[:, :tq - lo], acc[:, tq:tq + lo], acc_u[:, tq - lo:]], axis=1)
        else:
            m, acc = m_u, acc_u
    oT = acc[:MLA_V] / acc[MLA_V:MLA_V + 1]
    o_ref[...] = jnp.concatenate([oT[:, :tq], oT[:, tq:]], axis=0).T


def _mla(qmT, km, vmT, batch, seq, tq, tk, nsub=2):
    t = km.shape[0]
    nq = seq // tq
    npair = H_MLA // 2
    assert tk == tq and seq % tk == 0
    return pl.pallas_call(
        functools.partial(_mla_kernel, tq=tq, tk=tk, nsub=nsub),
        grid=(batch, npair, nq),
        in_specs=[pl.BlockSpec((2 * LANES, tq), lambda b, p, i: (p, b * nq + i)),
                  pl.BlockSpec((seq, 2 * LANES), lambda b, p, i: (b, p)),
                  pl.BlockSpec((seq // tk, LANES, tk), lambda b, p, i: (b, p, 0))],
        out_specs=pl.BlockSpec((tq, LANES), lambda b, p, i: (b * nq + i, p)),
        out_shape=jax.ShapeDtypeStruct((t, D_MLA), F32),
        compiler_params=pltpu.CompilerParams(dimension_semantics=("parallel", "parallel", "arbitrary"),
                                             vmem_limit_bytes=VMEM_LIMIT),
        name="mla",
    )(qmT, km, vmT)


def _moba_kernel(qT_ref, k_ref, vT_ref, kmean_ref, bias_ref, o_ref, pen_ref, *, nfar):
    blk = MOBA_BLOCK
    wide = H_MOBA * blk
    j = pl.program_id(1)
    nblk = kmean_ref.shape[0]
    qT = qT_ref[...]
    rhead = lax.broadcasted_iota(jnp.int32, (D_MOBA, blk), 0) // HEAD_DIM
    qsT = jnp.concatenate([jnp.where(rhead == h, qT, 0.0) for h in range(H_MOBA)], axis=1)

    gate = jnp.dot(kmean_ref[:, 0, :], qsT, preferred_element_type=F32, precision=lax.Precision.HIGHEST)
    bidx = lax.broadcasted_iota(jnp.int32, (nblk, wide), 0)
    gate = jnp.where(bidx < j, gate, -jnp.inf)
    pen = jnp.full((nblk, wide), NEG, F32)
    for _ in range(MOBA_TOPK):
        mx = jnp.max(gate, axis=0, keepdims=True)
        first = jnp.min(jnp.where(gate == mx, bidx, nblk), axis=0, keepdims=True)
        hit = bidx == first
        pen = jnp.where(hit & (mx > -jnp.inf), 0.0, pen)
        gate = jnp.where(hit, -jnp.inf, gate)

    pen_ref[0:nblk, :] = pen
    pen_ref[nblk:nblk + 1, :] = jnp.zeros((1, wide), F32)
    pen_ref[nblk + 1:nblk + 2, :] = jnp.full((1, wide), NEG, F32)
    rhs = qsT.astype(BF16)

    def scores(n):
        s = _dot(k_ref[pl.ds(pl.multiple_of(n * blk, blk), blk), :], rhs)
        return s + bias_ref[jnp.minimum(j - n, nfar)]

    ones = jnp.ones((ONES_ROWS, blk), BF16)

    def weighted(n, p):
        vT = vT_ref[n]
        return jnp.concatenate(
            [_dot(jnp.concatenate([vT[h * HEAD_DIM:(h + 1) * HEAD_DIM], ones], axis=0), p[:, h * blk:(h + 1) * blk])
             for h in range(H_MOBA)], axis=1)

    def pair(t, carry):
        m, acc = carry
        n0 = j - 2 * t
        n1 = jnp.maximum(n0 - 1, 0)
        s0, s1 = scores(n0), scores(n1)
        pick0 = jnp.where(t == 0, nblk, n0)
        pick1 = jnp.where(n0 == 0, nblk + 1, n1)
        for n, pick, s in ((n0, pick0, s0), (n1, pick1, s1)):
            pen = pen_ref[pl.ds(pick, 1), :]
            m_new = jnp.maximum(m, jnp.max(s, axis=0, keepdims=True) + pen)
            p = jnp.exp2(s - (m_new - pen)).astype(BF16)
            acc = jnp.exp2(m - m_new) * acc + weighted(n, p)
            m = m_new
        return m, acc

    init = (jnp.full((1, wide), NEG, F32), jnp.zeros((HEAD_DIM + ONES_ROWS, wide), F32))
    _, acc = lax.fori_loop(0, j // 2 + 1, pair, init)
    oT = acc[:HEAD_DIM] / acc[HEAD_DIM:HEAD_DIM + 1]
    o_ref[...] = jnp.concatenate([oT[:, h * blk:(h + 1) * blk] for h in range(H_MOBA)], axis=0).T


def _moba(qcT, kc, vcT, kmean, bias, batch, seq):
    t = kc.shape[0]
    nblk = seq // MOBA_BLOCK
    nfar = bias.shape[0] - 1
    return pl.pallas_call(
        functools.partial(_moba_kernel, nfar=nfar),
        grid=(batch, nblk),
        in_specs=[pl.BlockSpec((D_MOBA, MOBA_BLOCK), lambda b, j: (0, b * nblk + j)),
                  pl.BlockSpec((seq, D_MOBA), lambda b, j: (b, 0)),
                  pl.BlockSpec((nblk, D_MOBA, MOBA_BLOCK), lambda b, j: (b, 0, 0)),
                  pl.BlockSpec((nblk, 1, D_MOBA), lambda b, j: (b, 0, 0)),
                  pl.BlockSpec(bias.shape, lambda b, j: (0, 0, 0), pipeline_mode=pl.Buffered(1))],
        out_specs=pl.BlockSpec((MOBA_BLOCK, D_MOBA), lambda b, j: (b * nblk + j, 0)),
        out_shape=jax.ShapeDtypeStruct((t, D_MOBA), F32),
        scratch_shapes=[pltpu.VMEM((nblk + SUBLANES, H_MOBA * MOBA_BLOCK), F32)],
        compiler_params=pltpu.CompilerParams(dimension_semantics=("parallel", "arbitrary"),
                                             vmem_limit_bytes=VMEM_LIMIT),
        name="moba",
    )(qcT, kc, vcT, kmean, bias)


def _post_kernel(x_ref, oa_ref, ob_ref, oc_ref, gmix_ref, wo_ref, gmlp_ref, wup_ref, wdn_ref, gfin_ref,
                 y_ref, *, final, ff_chunk):
    gm = gmix_ref[...]
    x = x_ref[...]
    lo = 0
    for o_ref in (oa_ref, ob_ref, oc_ref):
        n = o_ref.shape[1]
        mixed = _rms(o_ref[...], gm[:, lo:lo + n]).astype(BF16)
        x = x + _dot(mixed, wo_ref[lo:lo + n, :])
        lo += n
    hb = _rms(x, gmlp_ref[...]).astype(BF16)
    y_ref[...] = x
    for c in range(wup_ref.shape[1] // ff_chunk):
        cs = slice(c * ff_chunk, (c + 1) * ff_chunk)
        u = jnp.maximum(_dot(hb, wup_ref[:, cs]), 0.0)
        y_ref[...] += _dot((u * u).astype(BF16), wdn_ref[cs, :])
    if final:
        y_ref[...] = _rms(y_ref[...], gfin_ref[...])


def _post(x, oa, ob, oc, layer, gmix, wo, gmlp, wup, wdn, gfin, final, tm=512, ff_chunk=1024):
    t, d = x.shape
    row = lambda i: (i, 0)
    stacked = lambda a: pl.BlockSpec((None,) + a.shape[1:], lambda i: (layer, 0, 0), pipeline_mode=pl.Buffered(1))
    return pl.pallas_call(
        functools.partial(_post_kernel, final=final, ff_chunk=ff_chunk),
        grid=(t // tm,),
        in_specs=[pl.BlockSpec((tm, d), row), pl.BlockSpec((tm, oa.shape[1]), row),
                  pl.BlockSpec((tm, ob.shape[1]), row), pl.BlockSpec((tm, oc.shape[1]), row),
                  stacked(gmix), stacked(wo), stacked(gmlp), stacked(wup), stacked(wdn),
                  pl.BlockSpec(gfin.shape, lambda i: (0, 0))],
        out_specs=pl.BlockSpec((tm, d), row),
        out_shape=jax.ShapeDtypeStruct((t, d), F32),
        compiler_params=pltpu.CompilerParams(dimension_semantics=("parallel",),
                                             vmem_limit_bytes=VMEM_LIMIT),
        name="post",
    )(x, oa, ob, oc, gmix, wo, gmlp, wup, wdn, gfin)


def _rope_tables(seq):
    inv_freq = ROPE_THETA ** (-jnp.arange(0, MLA_ROPE, 2, dtype=F32) / MLA_ROPE)
    ang = jnp.arange(seq, dtype=F32)[:, None] * inv_freq[None, :]
    cos, sin = jnp.cos(ang), jnp.sin(ang)
    half = MLA_ROPE // 2
    one = jnp.ones((seq, MLA_NOPE), F32)
    zero = jnp.zeros((seq, MLA_NOPE), F32)
    zh = jnp.zeros((seq, half), F32)
    tail1 = jnp.ones((seq, LANES - MLA_NOPE - MLA_ROPE), F32)
    tail0 = jnp.zeros((seq, LANES - MLA_NOPE - MLA_ROPE), F32)
    cos_t = jnp.concatenate([one, cos, cos, tail1], axis=1)
    sin_a = jnp.concatenate([zero, -sin, zh, tail0], axis=1)
    sin_b = jnp.concatenate([zero, zh, sin, tail0], axis=1)
    return cos_t, sin_a, sin_b, cos.T, sin.T


def _lookup(tab, bucket):
    col = lambda b: tab[b][(slice(None),) + (None,) * bucket.ndim]
    out = jnp.broadcast_to(col(0), (tab.shape[1],) + bucket.shape)
    for b in range(1, NUM_BUCKETS):
        out = jnp.where(bucket[None] == b, col(b), out)
    return out.astype(F32)


def _toeplitz(v, rows, cols):
    lead = v.shape[:-1]
    period = rows + cols
    u = jnp.pad(v[..., ::-1], [(0, 0)] * len(lead) + [(0, 1)])
    flat = jnp.tile(u, (1,) * len(lead) + (rows,))[..., :rows * (period - 1)]
    return flat.reshape(lead + (rows, period - 1))[..., rows - 1:rows - 1 + cols]


def _dil_bias(bias_tab):
    span = DIL_SPAN
    npat = len(DILATED_PATTERNS)
    diff = jnp.arange(3 * span - 1) - (span - 1)
    in_band = (diff >= 0) & (diff <= span)
    tabs = []
    for _, dil in DILATED_PATTERNS:
        by_diff = jnp.where(in_band[None], _lookup(bias_tab, _bucket(diff * dil)) * LOG2E, NEG)
        tabs.append(_toeplitz(by_diff, span, 2 * span))
    b = jnp.stack(tabs, axis=1)
    b = b.reshape(H_DIL // 2, 2, npat, span, 2 * span).transpose(0, 2, 1, 3, 4)
    return b.reshape(H_DIL // 2, npat, 2 * span, 2 * span)


def _moba_bias(bias_tab, nblk):
    blk = MOBA_BLOCK
    nfar = min(nblk - 1, REL_MAX_DISTANCE // blk + 1)
    dist = jnp.arange((nfar + 2) * blk - 1) - (blk - 1)
    by_dist = jnp.where(dist[None] >= 0, _lookup(bias_tab, _bucket(dist)) * LOG2E, NEG)
    windows = jnp.stack([by_dist[:, d * blk:(d + 2) * blk - 1] for d in range(nfar + 1)], axis=1)
    b = _toeplitz(windows, blk, blk)
    return b.transpose(1, 3, 0, 2).reshape(nfar + 1, blk, H_MOBA * blk)


def _pad_w_in(w):
    sizes = (D_DIL, D_DIL, D_DIL, MLA_Q_LORA, MLA_KV_LORA, MLA_ROPE, D_MOBA, D_MOBA, D_MOBA)
    parts, lo = [], 0
    for n in sizes:
        parts.append(w[..., lo:lo + n])
        lo += n
    zeros = lambda n: jnp.zeros(w.shape[:-1] + (n,), w.dtype)
    parts[5] = jnp.concatenate([zeros(MLA_NOPE), parts[5], zeros(LANES - MLA_NOPE - MLA_ROPE)], axis=-1)
    return jnp.concatenate(parts, axis=-1).astype(BF16)


def _pad_w_uq_t(w):
    depth, r, _ = w.shape
    w = w.reshape(depth, r, H_MLA, MLA_NOPE + MLA_ROPE)
    w = jnp.pad(w, ((0, 0), (0, 0), (0, 0), (0, LANES - MLA_NOPE - MLA_ROPE)))
    return w.reshape(depth, r, H_MLA * LANES).transpose(0, 2, 1).astype(BF16)


def _split_w_ukv(w):
    depth, r, _ = w.shape
    w = w.reshape(depth, r, H_MLA, MLA_NOPE + MLA_V)
    wk = jnp.pad(w[..., :MLA_NOPE], ((0, 0), (0, 0), (0, 0), (0, LANES - MLA_NOPE)))
    wv = w[..., MLA_NOPE:].reshape(depth, r, H_MLA * MLA_V)
    return wk.reshape(depth, r, H_MLA * LANES).astype(BF16), wv.transpose(0, 2, 1).astype(BF16)


def kernel(x, g_attn, w_in, g_q_lora, g_kv_lora, w_uq, w_ukv, rel_bias, g_mix, w_o, g_mlp, w_up, w_down,
           g_final):
    batch, seq, d = x.shape
    depth = w_in.shape[0]
    assert seq % (DIL_SPAN * max(dil for _, dil in DILATED_PATTERNS)) == 0
    assert seq % IN_TILE == 0 and IN_TILE % MLA_TK == 0
    cos_t, sin_a, sin_b, cosT, sinT = _rope_tables(seq)
    bias_dil = _dil_bias(rel_bias[:, :H_DIL])
    bias_moba = _moba_bias(rel_bias[:, H_DIL:], seq // MOBA_BLOCK)
    rows = lambda v: v[:, None, :].astype(F32)
    w = _pad_w_in(w_in)
    wk, wvT = _split_w_ukv(w_ukv)
    in_params = (rows(g_attn), w, rows(g_q_lora), rows(g_kv_lora), _pad_w_uq_t(w_uq), wk, wvT,
                 w[:, :, C_QC:C_QC + D_MOBA].transpose(0, 2, 1), w[:, :, C_VC:C_VC + D_MOBA].transpose(0, 2, 1))
    post_params = (rows(g_mix), w_o.astype(BF16), rows(g_mlp), w_up.astype(BF16), w_down.astype(BF16))
    gfin = g_final.reshape(1, -1).astype(F32)
    xf = x.reshape(batch * seq, d)
    for l in range(depth):
        qa, ka, va, qmT, km, vmT, qcT, kc, vcT, kmean = _in_proj(
            xf, l, *in_params, cos_t, sin_a, sin_b, cosT, sinT, seq, tm=IN_TILE)
        oa = _dilated(qa, ka, va, bias_dil, batch, seq)
        ob = _mla(qmT, km, vmT, batch, seq, tq=MLA_TQ, tk=MLA_TK)
        oc = _moba(qcT, kc, vcT, kmean, bias_moba, batch, seq)
        xf = _post(xf, oa, ob, oc, l, *post_params, gfin, final=(l == depth - 1))
    return xf.reshape(batch, seq, d)
```

```python
import functools
import math

import jax
import jax.numpy as jnp
from jax import lax
from jax.experimental import pallas as pl
from jax.experimental.pallas import tpu as pltpu

LANES = 128
HEAD_DIM = 64
H_DIL, H_MLA, H_MOBA = 6, 6, 4
D_DIL = H_DIL * HEAD_DIM
MLA_NOPE, MLA_ROPE, MLA_V = 64, 32, 64
MLA_Q_LORA, MLA_KV_LORA = 384, 128
D_MLA = H_MLA * MLA_V
D_MOBA = H_MOBA * HEAD_DIM
DILATED_PATTERNS = ((128, 1), (512, 4), (2048, 16))
DIL_SPAN = 128
MOBA_BLOCK = 256
MOBA_TOPK = 3
MOBA_QBLOCKS = 2
NUM_BUCKETS = 32
MAX_EXACT = 16
REL_MAX_DISTANCE = 2048
ROPE_THETA = 10000.0
EPS = 1e-6
NEG = -1e30
LOG2E = math.log2(math.e)
ONES_ROWS = 16
VMEM_LIMIT = 56 * 1024 * 1024
IN_TILE = 1024
MLA_TQ = 512
MLA_TK = 512

C_QA, C_KA, C_VA = 0, 384, 768
C_CQ, C_CKV, C_KR = 1152, 1536, 1664
C_QC, C_KC, C_VC = 1792, 2048, 2304
D_IN_PAD = 2560

BF16 = jnp.bfloat16
F32 = jnp.float32


def _bucket(dist):
    n = jnp.maximum(dist, 0)
    nf = jnp.maximum(n, 1).astype(F32)
    large = MAX_EXACT + (jnp.log(nf / MAX_EXACT) / math.log(REL_MAX_DISTANCE / MAX_EXACT)
                         * (NUM_BUCKETS - MAX_EXACT)).astype(jnp.int32)
    large = jnp.minimum(large, NUM_BUCKETS - 1)
    return jnp.where(n < MAX_EXACT, n, large)


def _rms(x, g):
    return x * lax.rsqrt(jnp.mean(x * x, axis=-1, keepdims=True) + EPS) * g


def _dot(a, b):
    return jnp.dot(a, b, preferred_element_type=F32)


def _dot_nt(a, b, precision=None):
    return lax.dot_general(a, b, (((1,), (1,)), ((), ())), preferred_element_type=F32,
                           precision=precision)


def _rope(x, cos, sin_a, sin_b):
    return x * cos + pltpu.roll(x, LANES - 16, 1) * sin_a + pltpu.roll(x, 16, 1) * sin_b


def _in_kernel(x_ref, g_ref, w_ref, gq_ref, gkv_ref, wuqT_ref, wk_ref, wvT_ref, wqcT_ref, wvcT_ref,
               cos_ref, sa_ref, sb_ref, cosT_ref, sinT_ref,
               qa_ref, ka_ref, va_ref, qmT_ref, km_ref, vmT_ref, qcT_ref, kc_ref, vcT_ref, kmean_ref):
    hb = _rms(x_ref[...], g_ref[...]).astype(BF16)

    def proj(c0, n):
        return _dot(hb, w_ref[:, c0:c0 + n])

    qa_ref[...] = proj(C_QA, D_DIL) * (HEAD_DIM ** -0.5 * LOG2E)
    ka_ref[...] = proj(C_KA, D_DIL)
    va_ref[...] = proj(C_VA, D_DIL)

    cq = _rms(proj(C_CQ, MLA_Q_LORA), gq_ref[...]).astype(BF16)
    qT = _dot_nt(wuqT_ref[...], cq)
    cosT, sinT = cosT_ref[...], sinT_ref[...]
    scale = (MLA_NOPE + MLA_ROPE) ** -0.5 * LOG2E
    half = MLA_ROPE // 2
    for h in range(H_MLA):
        r0 = h * LANES
        x1 = qT[r0 + MLA_NOPE:r0 + MLA_NOPE + half]
        x2 = qT[r0 + MLA_NOPE + half:r0 + MLA_NOPE + MLA_ROPE]
        roped = jnp.concatenate([qT[r0:r0 + MLA_NOPE], x1 * cosT - x2 * sinT, x2 * cosT + x1 * sinT,
                                 qT[r0 + MLA_NOPE + MLA_ROPE:r0 + LANES]], axis=0)
        qmT_ref[r0:r0 + LANES, :] = (roped * scale).astype(BF16)
    ckv = _rms(proj(C_CKV, MLA_KV_LORA), gkv_ref[...]).astype(BF16)
    kn = _dot(ckv, wk_ref[...])
    vmT = _dot_nt(wvT_ref[...], ckv).astype(BF16)
    for i in range(vmT_ref.shape[0]):
        vmT_ref[i] = vmT[:, i * MLA_TK:(i + 1) * MLA_TK]
    kr = _rope(proj(C_KR, LANES), cos_ref[...], sa_ref[...], sb_ref[...])
    for h in range(H_MLA):
        sl = slice(h * LANES, (h + 1) * LANES)
        km_ref[:, sl] = (kn[:, sl] + kr).astype(BF16)

    qcT_ref[...] = _dot_nt(wqcT_ref[...], hb) * (HEAD_DIM ** -0.5 * LOG2E)
    kc = proj(C_KC, D_MOBA)
    kc_ref[...] = kc.astype(BF16)
    vcT = _dot_nt(wvcT_ref[...], hb).astype(BF16)
    for i in range(kc.shape[0] // MOBA_BLOCK):
        sl = slice(i * MOBA_BLOCK, (i + 1) * MOBA_BLOCK)
        vcT_ref[i] = vcT[:, sl]
        kmean_ref[i] = jnp.mean(kc[sl], axis=0, keepdims=True)


def _in_proj(x, layer, g, w, gq, gkv, wuqT, wk, wvT, wqcT, wvcT, cos, sin_a, sin_b, cosT, sinT, seq, tm):
    t, d = x.shape
    nt = seq // tm
    row = lambda i: (i, 0)
    col = lambda i: (0, i)
    lead = lambda i: (i, 0, 0)
    pos = lambda i: (i % nt, 0)
    posT = lambda i: (0, i % nt)
    full = lambda a: pl.BlockSpec((None,) + a.shape[1:], lambda i: (layer, 0, 0), pipeline_mode=pl.Buffered(1))
    nb = tm // MOBA_BLOCK
    outs = [
        ((t, D_DIL), F32, (tm, D_DIL), row), ((t, D_DIL), F32, (tm, D_DIL), row),
        ((t, D_DIL), F32, (tm, D_DIL), row),
        ((H_MLA * LANES, t), BF16, (H_MLA * LANES, tm), col),
        ((t, H_MLA * LANES), BF16, (tm, H_MLA * LANES), row),
        ((t // MLA_TK, D_MLA, MLA_TK), BF16, (tm // MLA_TK, D_MLA, MLA_TK), lead),
        ((D_MOBA, t), F32, (D_MOBA, tm), col),
        ((t, D_MOBA), BF16, (tm, D_MOBA), row),
        ((t // MOBA_BLOCK, D_MOBA, MOBA_BLOCK), BF16, (nb, D_MOBA, MOBA_BLOCK), lead),
        ((t // MOBA_BLOCK, 1, D_MOBA), F32, (nb, 1, D_MOBA), lead),
    ]
    return pl.pallas_call(
        _in_kernel,
        grid=(t // tm,),
        in_specs=[pl.BlockSpec((tm, d), row), full(g), full(w), full(gq), full(gkv), full(wuqT), full(wk),
                  full(wvT), full(wqcT), full(wvcT),
                  pl.BlockSpec((tm, LANES), pos), pl.BlockSpec((tm, LANES), pos), pl.BlockSpec((tm, LANES), pos),
                  pl.BlockSpec((MLA_ROPE // 2, tm), posT), pl.BlockSpec((MLA_ROPE // 2, tm), posT)],
        out_specs=[pl.BlockSpec(blk, imap) for _, _, blk, imap in outs],
        out_shape=[jax.ShapeDtypeStruct(shape, dt) for shape, dt, _, _ in outs],
        compiler_params=pltpu.CompilerParams(dimension_semantics=("parallel",),
                                             vmem_limit_bytes=VMEM_LIMIT),
        name="in_proj",
    )(x, g, w, gq, gkv, wuqT, wk, wvT, wqcT, wvcT, cos, sin_a, sin_b, cosT, sinT)


def _dil_kernel(q_ref, k_ref, v_ref, bias_ref, o_ref, oscr, lscr, *, unroll):
    seq = q_ref.shape[0]
    span = DIL_SPAN
    lane = lax.broadcasted_iota(jnp.int32, (span, LANES), 1)
    head0 = lane < HEAD_DIM

    for pi, (_, dil) in enumerate(DILATED_PATTERNS):
        nblk = seq // (span * dil)

        def rows(start, dil=dil):
            return pl.ds(start, span) if dil == 1 else pl.ds(start, span, stride=dil)

        def unit(u, carry, first, pi=pi, dil=dil, rows=rows):
            r = u if first else u % dil
            cur = r if first else (u // dil) * (span * dil) + r
            q = q_ref[rows(cur), :]
            qs = jnp.concatenate([jnp.where(head0, q, 0.0), jnp.where(head0, 0.0, q)], axis=0).astype(BF16)
            if first:
                kk = k_ref[rows(cur), :].astype(BF16)
                vv = v_ref[rows(cur), :].astype(BF16)
                s = _dot_nt(qs, kk) + bias_ref[pi, :, span:]
            else:
                prev = cur - span * dil
                kk = jnp.concatenate([k_ref[rows(prev), :], k_ref[rows(cur), :]], axis=0).astype(BF16)
                vv = jnp.concatenate([v_ref[rows(prev), :], v_ref[rows(cur), :]], axis=0).astype(BF16)
                s = _dot_nt(qs, kk) + bias_ref[pi]
            m = jnp.max(s, axis=-1, keepdims=True)
            p = jnp.exp2(s - m).astype(BF16)
            on = _dot(p, jnp.concatenate([vv, jnp.ones_like(vv)], axis=1))
            l = on[:, LANES:]
            o = on[:, :LANES] / l
            lse = m + jnp.log(l) * LOG2E
            oscr[pi, rows(cur), :] = jnp.where(head0, o[:span], o[span:])
            lscr[pi, rows(cur), :] = jnp.where(head0, lse[:span], lse[span:])
            return carry

        lax.fori_loop(0, dil, functools.partial(unit, first=True), 0, unroll=min(unroll, dil))
        lax.fori_loop(dil, nblk * dil, functools.partial(unit, first=False), 0, unroll=unroll)

    ct = 512
    for c in range(seq // ct):
        sl = pl.ds(c * ct, ct)
        l0, l1, l2 = lscr[0, sl, :], lscr[1, sl, :], lscr[2, sl, :]
        mx = jnp.maximum(jnp.maximum(l0, l1), l2)
        e0, e1, e2 = jnp.exp2(l0 - mx), jnp.exp2(l1 - mx), jnp.exp2(l2 - mx)
        num = e0 * oscr[0, sl, :] + e1 * oscr[1, sl, :] + e2 * oscr[2, sl, :]
        o_ref[sl, :] = num / (e0 + e1 + e2)


def _dilated(qa, ka, va, bias, batch, seq, unroll=8):
    t = qa.shape[0]
    npair = H_DIL // 2
    blk = pl.BlockSpec((seq, LANES), lambda b, p: (b, p))
    return pl.pallas_call(
        functools.partial(_dil_kernel, unroll=unroll),
        grid=(batch, npair),
        in_specs=[blk, blk, blk,
                  pl.BlockSpec((None, len(DILATED_PATTERNS), 2 * DIL_SPAN, 2 * DIL_SPAN),
                               lambda b, p: (p, 0, 0, 0))],
        out_specs=blk,
        out_shape=jax.ShapeDtypeStruct((t, D_DIL), F32),
        scratch_shapes=[pltpu.VMEM((len(DILATED_PATTERNS), seq, LANES), F32),
                        pltpu.VMEM((len(DILATED_PATTERNS), seq, LANES), F32)],
        compiler_params=pltpu.CompilerParams(dimension_semantics=("parallel", "parallel"),
                                             vmem_limit_bytes=VMEM_LIMIT),
        name="dilated",
    )(qa, ka, va, bias)


def _mla_kernel(qT_ref, k_ref, vT_ref, o_ref, *, tq, tk, nsub):
    i = pl.program_id(2)
    sub = tk // nsub
    zero = jnp.zeros((LANES, tq), BF16)
    rhs = jnp.concatenate([jnp.concatenate([qT_ref[:LANES, :], zero], axis=1),
                           jnp.concatenate([zero, qT_ref[LANES:, :]], axis=1)], axis=0)
    ones = jnp.ones((ONES_ROWS, sub), BF16)

    def both(x, lo):
        return x if lo == 0 else jnp.concatenate([x[:, lo:tq], x[:, tq + lo:]], axis=1)

    def scores(j, u, lo, diag):
        w = tq - lo
        s = _dot(k_ref[pl.ds(pl.multiple_of(j * tk + u * sub, sub), sub), :], both(rhs, lo))
        if diag:
            krow = lax.broadcasted_iota(jnp.int32, (sub, 2 * w), 0) + u * sub
            qcol = lax.broadcasted_iota(jnp.int32, (sub, 2 * w), 1)
            qcol = jnp.where(qcol >= w, qcol - w, qcol) + lo
            s = jnp.where(krow <= qcol, s, NEG)
        return s

    def update(j, u, s, m, acc):
        w = s.shape[1] // 2
        m_new = jnp.maximum(m, jnp.max(s, axis=0, keepdims=True))
        p = jnp.exp2(s - m_new).astype(BF16)
        vu = vT_ref[j][:, u * sub:(u + 1) * sub]
        pv = jnp.concatenate([_dot(jnp.concatenate([vu[:MLA_V], ones], axis=0), p[:, :w]),
                              _dot(jnp.concatenate([vu[MLA_V:], ones], axis=0), p[:, w:])], axis=1)
        return m_new, jnp.exp2(m - m_new) * acc + pv

    def past(tiles, carry):
        ss = [(j, u, scores(j, u, 0, False)) for j in tiles for u in range(nsub)]
        for j, u, s in ss:
            carry = update(j, u, s, *carry)
        return carry

    init = (jnp.full((1, 2 * tq), NEG, F32), jnp.zeros((MLA_V + ONES_ROWS, 2 * tq), F32))
    carry = lax.fori_loop(0, i // 2, lambda t, c: past((2 * t, 2 * t + 1), c), init)
    m, acc = lax.fori_loop(2 * (i // 2), i, lambda j, c: past((j,), c), carry)
    ss = [scores(i, u, u * sub, True) for u in range(nsub)]
    for u, s in enumerate(ss):
        lo = u * sub
        m_u, acc_u = update(i, u, s, both(m, lo), both(acc, lo))
        if lo:
            m = jnp.concatenate([m[:, :lo], m_u[:, :tq - lo], m[:, tq:tq + lo], m_u[:, tq - lo:]], axis=1)
            acc = jnp.concatenate([acc[:, :lo], acc_u[:, :tq - lo], acc[:, tq:tq + lo], acc_u[:, tq - lo:]], axis=1)
        else:
            m, acc = m_u, acc_u
    oT = acc[:MLA_V] / acc[MLA_V:MLA_V + 1]
    o_ref[...] = jnp.concatenate([oT[:, :tq], oT[:, tq:]], axis=0).T


def _mla(qmT, km, vmT, batch, seq, tq, tk, nsub=2):
    t = km.shape[0]
    nq = seq // tq
    npair = H_MLA // 2
    assert tk == tq and seq % tk == 0
    return pl.pallas_call(
        functools.partial(_mla_kernel, tq=tq, tk=tk, nsub=nsub),
        grid=(batch, npair, nq),
        in_specs=[pl.BlockSpec((2 * LANES, tq), lambda b, p, i: (p, b * nq + i)),
                  pl.BlockSpec((seq, 2 * LANES), lambda b, p, i: (b, p)),
                  pl.BlockSpec((seq // tk, LANES, tk), lambda b, p, i: (b, p, 0))],
        out_specs=pl.BlockSpec((tq, LANES), lambda b, p, i: (b * nq + i, p)),
        out_shape=jax.ShapeDtypeStruct((t, D_MLA), F32),
        compiler_params=pltpu.CompilerParams(dimension_semantics=("parallel", "parallel", "arbitrary"),
                                             vmem_limit_bytes=VMEM_LIMIT),
        name="mla",
    )(qmT, km, vmT)


def _moba_kernel(qT_ref, k_ref, vT_ref, kmean_ref, bias_ref, o_ref, pen_ref, *, nfar):
    blk = MOBA_BLOCK
    qb = MOBA_QBLOCKS
    per_head = qb * blk
    wide = H_MOBA * per_head
    j0 = pl.program_id(1) * qb
    nblk = kmean_ref.shape[0]
    qT = qT_ref[...]
    rhead = lax.broadcasted_iota(jnp.int32, (D_MOBA, per_head), 0) // HEAD_DIM
    qsT = jnp.concatenate([jnp.where(rhead == h, qT, 0.0) for h in range(H_MOBA)], axis=1)

    gate = jnp.dot(kmean_ref[:, 0, :], qsT, preferred_element_type=F32, precision=lax.Precision.HIGHEST)
    bidx = lax.broadcasted_iota(jnp.int32, (nblk, wide), 0)
    own = j0 + (lax.broadcasted_iota(jnp.int32, (nblk, wide), 1) % per_head) // blk
    gate = jnp.where(bidx < own, gate, -jnp.inf)
    pen = jnp.full((nblk, wide), 2 * NEG, F32)
    for _ in range(MOBA_TOPK):
        mx = jnp.max(gate, axis=0, keepdims=True)
        first = jnp.min(jnp.where(gate == mx, bidx, nblk), axis=0, keepdims=True)
        hit = bidx == first
        pen = jnp.where(hit & (mx > -jnp.inf), 0.0, pen)
        gate = jnp.where(hit, -jnp.inf, gate)

    pen_ref[...] = jnp.where(bidx == own, 0.0, pen)
    rhs = qsT.astype(BF16)

    def scores(n):
        s = _dot(k_ref[pl.ds(pl.multiple_of(n * blk, blk), blk), :], rhs)
        tiles = [bias_ref[jnp.clip(j0 + c - n, 0, nfar)] for c in range(qb)]
        bias = jnp.concatenate([tiles[c][:, h * blk:(h + 1) * blk] for h in range(H_MOBA) for c in range(qb)],
                               axis=1)
        return s + bias

    ones = jnp.ones((ONES_ROWS, blk), BF16)

    def weighted(n, p):
        vT = vT_ref[n]
        return jnp.concatenate(
            [_dot(jnp.concatenate([vT[h * HEAD_DIM:(h + 1) * HEAD_DIM], ones], axis=0),
                  p[:, h * per_head:(h + 1) * per_head]) for h in range(H_MOBA)], axis=1)

    def pair(t, carry):
        m, acc = carry
        n0 = j0 + qb - 1 - 2 * t
        n1 = n0 - 1
        s0, s1 = scores(n0), scores(n1)
        for n, s in ((n0, s0), (n1, s1)):
            pen = pen_ref[pl.ds(n, 1), :]
            m_new = jnp.maximum(m, jnp.max(s, axis=0, keepdims=True) + pen)
            p = jnp.exp2(s - (m_new - pen)).astype(BF16)
            acc = jnp.exp2(m - m_new) * acc + weighted(n, p)
            m = m_new
        return m, acc

    init = (jnp.full((1, wide), NEG, F32), jnp.zeros((HEAD_DIM + ONES_ROWS, wide), F32))
    _, acc = lax.fori_loop(0, (j0 + qb) // 2, pair, init)
    oT = acc[:HEAD_DIM] / acc[HEAD_DIM:HEAD_DIM + 1]
    o_ref[...] = jnp.concatenate([oT[:, h * per_head:(h + 1) * per_head] for h in range(H_MOBA)], axis=0).T


def _moba(qcT, kc, vcT, kmean, bias, batch, seq):
    t = kc.shape[0]
    nblk = seq // MOBA_BLOCK
    nstep = nblk // MOBA_QBLOCKS
    rows = MOBA_QBLOCKS * MOBA_BLOCK
    nfar = bias.shape[0] - 1
    assert MOBA_QBLOCKS % 2 == 0 and nblk % MOBA_QBLOCKS == 0
    return pl.pallas_call(
        functools.partial(_moba_kernel, nfar=nfar),
        grid=(batch, nstep),
        in_specs=[pl.BlockSpec((D_MOBA, rows), lambda b, j: (0, b * nstep + j)),
                  pl.BlockSpec((seq, D_MOBA), lambda b, j: (b, 0)),
                  pl.BlockSpec((nblk, D_MOBA, MOBA_BLOCK), lambda b, j: (b, 0, 0)),
                  pl.BlockSpec((nblk, 1, D_MOBA), lambda b, j: (b, 0, 0)),
                  pl.BlockSpec(bias.shape, lambda b, j: (0, 0, 0), pipeline_mode=pl.Buffered(1))],
        out_specs=pl.BlockSpec((rows, D_MOBA), lambda b, j: (b * nstep + j, 0)),
        out_shape=jax.ShapeDtypeStruct((t, D_MOBA), F32),
        scratch_shapes=[pltpu.VMEM((nblk, H_MOBA * rows), F32)],
        compiler_params=pltpu.CompilerParams(dimension_semantics=("parallel", "arbitrary"),
                                             vmem_limit_bytes=VMEM_LIMIT),
        name="moba",
    )(qcT, kc, vcT, kmean, bias)


def _post_kernel(x_ref, oa_ref, ob_ref, oc_ref, gmix_ref, wo_ref, gmlp_ref, wup_ref, wdn_ref, gfin_ref,
                 y_ref, *, final, ff_chunk):
    gm = gmix_ref[...]
    x = x_ref[...]
    lo = 0
    for o_ref in (oa_ref, ob_ref, oc_ref):
        n = o_ref.shape[1]
        mixed = _rms(o_ref[...], gm[:, lo:lo + n]).astype(BF16)
        x = x + _dot(mixed, wo_ref[lo:lo + n, :])
        lo += n
    hb = _rms(x, gmlp_ref[...]).astype(BF16)
    y_ref[...] = x
    for c in range(wup_ref.shape[1] // ff_chunk):
        cs = slice(c * ff_chunk, (c + 1) * ff_chunk)
        u = jnp.maximum(_dot(hb, wup_ref[:, cs]), 0.0)
        y_ref[...] += _dot((u * u).astype(BF16), wdn_ref[cs, :])
    if final:
        y_ref[...] = _rms(y_ref[...], gfin_ref[...])


def _post(x, oa, ob, oc, layer, gmix, wo, gmlp, wup, wdn, gfin, final, tm=512, ff_chunk=1024):
    t, d = x.shape
    row = lambda i: (i, 0)
    stacked = lambda a: pl.BlockSpec((None,) + a.shape[1:], lambda i: (layer, 0, 0), pipeline_mode=pl.Buffered(1))
    return pl.pallas_call(
        functools.partial(_post_kernel, final=final, ff_chunk=ff_chunk),
        grid=(t // tm,),
        in_specs=[pl.BlockSpec((tm, d), row), pl.BlockSpec((tm, oa.shape[1]), row),
                  pl.BlockSpec((tm, ob.shape[1]), row), pl.BlockSpec((tm, oc.shape[1]), row),
                  stacked(gmix), stacked(wo), stacked(gmlp), stacked(wup), stacked(wdn),
                  pl.BlockSpec(gfin.shape, lambda i: (0, 0))],
        out_specs=pl.BlockSpec((tm, d), row),
        out_shape=jax.ShapeDtypeStruct((t, d), F32),
        compiler_params=pltpu.CompilerParams(dimension_semantics=("parallel",),
                                             vmem_limit_bytes=VMEM_LIMIT),
        name="post",
    )(x, oa, ob, oc, gmix, wo, gmlp, wup, wdn, gfin)


def _rope_tables(seq):
    inv_freq = ROPE_THETA ** (-jnp.arange(0, MLA_ROPE, 2, dtype=F32) / MLA_ROPE)
    ang = jnp.arange(seq, dtype=F32)[:, None] * inv_freq[None, :]
    cos, sin = jnp.cos(ang), jnp.sin(ang)
    half = MLA_ROPE // 2
    one = jnp.ones((seq, MLA_NOPE), F32)
    zero = jnp.zeros((seq, MLA_NOPE), F32)
    zh = jnp.zeros((seq, half), F32)
    tail1 = jnp.ones((seq, LANES - MLA_NOPE - MLA_ROPE), F32)
    tail0 = jnp.zeros((seq, LANES - MLA_NOPE - MLA_ROPE), F32)
    cos_t = jnp.concatenate([one, cos, cos, tail1], axis=1)
    sin_a = jnp.concatenate([zero, -sin, zh, tail0], axis=1)
    sin_b = jnp.concatenate([zero, zh, sin, tail0], axis=1)
    return cos_t, sin_a, sin_b, cos.T, sin.T


def _lookup(tab, bucket):
    col = lambda b: tab[b][(slice(None),) + (None,) * bucket.ndim]
    out = jnp.broadcast_to(col(0), (tab.shape[1],) + bucket.shape)
    for b in range(1, NUM_BUCKETS):
        out = jnp.where(bucket[None] == b, col(b), out)
    return out.astype(F32)


def _toeplitz(v, rows, cols):
    lead = v.shape[:-1]
    period = rows + cols
    u = jnp.pad(v[..., ::-1], [(0, 0)] * len(lead) + [(0, 1)])
    flat = jnp.tile(u, (1,) * len(lead) + (rows,))[..., :rows * (period - 1)]
    return flat.reshape(lead + (rows, period - 1))[..., rows - 1:rows - 1 + cols]


def _dil_bias(bias_tab):
    span = DIL_SPAN
    npat = len(DILATED_PATTERNS)
    diff = jnp.arange(3 * span - 1) - (span - 1)
    in_band = (diff >= 0) & (diff <= span)
    tabs = []
    for _, dil in DILATED_PATTERNS:
        by_diff = jnp.where(in_band[None], _lookup(bias_tab, _bucket(diff * dil)) * LOG2E, NEG)
        tabs.append(_toeplitz(by_diff, span, 2 * span))
    b = jnp.stack(tabs, axis=1)
    b = b.reshape(H_DIL // 2, 2, npat, span, 2 * span).transpose(0, 2, 1, 3, 4)
    return b.reshape(H_DIL // 2, npat, 2 * span, 2 * span)


def _moba_bias(bias_tab, nblk):
    blk = MOBA_BLOCK
    nfar = min(nblk - 1, REL_MAX_DISTANCE // blk + 1)
    dist = jnp.arange((nfar + 2) * blk - 1) - (blk - 1)
    by_dist = jnp.where(dist[None] >= 0, _lookup(bias_tab, _bucket(dist)) * LOG2E, NEG)
    windows = jnp.stack([by_dist[:, d * blk:(d + 2) * blk - 1] for d in range(nfar + 1)], axis=1)
    b = _toeplitz(windows, blk, blk)
    return b.transpose(1, 3, 0, 2).reshape(nfar + 1, blk, H_MOBA * blk)


def _pad_w_in(w):
    sizes = (D_DIL, D_DIL, D_DIL, MLA_Q_LORA, MLA_KV_LORA, MLA_ROPE, D_MOBA, D_MOBA, D_MOBA)
    parts, lo = [], 0
    for n in sizes:
        parts.append(w[..., lo:lo + n])
        lo += n
    zeros = lambda n: jnp.zeros(w.shape[:-1] + (n,), w.dtype)
    parts[5] = jnp.concatenate([zeros(MLA_NOPE), parts[5], zeros(LANES - MLA_NOPE - MLA_ROPE)], axis=-1)
    return jnp.concatenate(parts, axis=-1).astype(BF16)


def _pad_w_uq_t(w):
    depth, r, _ = w.shape
    w = w.reshape(depth, r, H_MLA, MLA_NOPE + MLA_ROPE)
    w = jnp.pad(w, ((0, 0), (0, 0), (0, 0), (0, LANES - MLA_NOPE - MLA_ROPE)))
    return w.reshape(depth, r, H_MLA * LANES).transpose(0, 2, 1).astype(BF16)


def _split_w_ukv(w):
    depth, r, _ = w.shape
    w = w.reshape(depth, r, H_MLA, MLA_NOPE + MLA_V)
    wk = jnp.pad(w[..., :MLA_NOPE], ((0, 0), (0, 0), (0, 0), (0, LANES - MLA_NOPE)))
    wv = w[..., MLA_NOPE:].reshape(depth, r, H_MLA * MLA_V)
    return wk.reshape(depth, r, H_MLA * LANES).astype(BF16), wv.transpose(0, 2, 1).astype(BF16)


def kernel(x, g_attn, w_in, g_q_lora, g_kv_lora, w_uq, w_ukv, rel_bias, g_mix, w_o, g_mlp, w_up, w_down,
           g_final):
    batch, seq, d = x.shape
    depth = w_in.shape[0]
    assert seq % (DIL_SPAN * max(dil for _, dil in DILATED_PATTERNS)) == 0
    assert seq % IN_TILE == 0 and IN_TILE % MLA_TK == 0
    cos_t, sin_a, sin_b, cosT, sinT = _rope_tables(seq)
    bias_dil = _dil_bias(rel_bias[:, :H_DIL])
    bias_moba = _moba_bias(rel_bias[:, H_DIL:], seq // MOBA_BLOCK)
    rows = lambda v: v[:, None, :].astype(F32)
    w = _pad_w_in(w_in)
    wk, wvT = _split_w_ukv(w_ukv)
    in_params = (rows(g_attn), w, rows(g_q_lora), rows(g_kv_lora), _pad_w_uq_t(w_uq), wk, wvT,
                 w[:, :, C_QC:C_QC + D_MOBA].transpose(0, 2, 1), w[:, :, C_VC:C_VC + D_MOBA].transpose(0, 2, 1))
    post_params = (rows(g_mix), w_o.astype(BF16), rows(g_mlp), w_up.astype(BF16), w_down.astype(BF16))
    gfin = g_final.reshape(1, -1).astype(F32)
    xf = x.reshape(batch * seq, d)
    for l in range(depth):
        qa, ka, va, qmT, km, vmT, qcT, kc, vcT, kmean = _in_proj(
            xf, l, *in_params, cos_t, sin_a, sin_b, cosT, sinT, seq, tm=IN_TILE)
        oa = _dilated(qa, ka, va, bias_dil, batch, seq)
        ob = _mla(qmT, km, vmT, batch, seq, tq=MLA_TQ, tk=MLA_TK)
        oc = _moba(qcT, kc, vcT, kmean, bias_moba, batch, seq)
        xf = _post(xf, oa, ob, oc, l, *post_params, gfin, final=(l == depth - 1))
    return xf.reshape(batch, seq, d)
```

```python
import functools
import math

import jax
import jax.numpy as jnp
from jax import lax
from jax.experimental import pallas as pl
from jax.experimental.pallas import tpu as pltpu

LANES = 128
HEAD_DIM = 64
H_DIL, H_MLA, H_MOBA = 6, 6, 4
D_DIL = H_DIL * HEAD_DIM
MLA_NOPE, MLA_ROPE, MLA_V = 64, 32, 64
MLA_Q_LORA, MLA_KV_LORA = 384, 128
D_MLA = H_MLA * MLA_V
D_MOBA = H_MOBA * HEAD_DIM
DILATED_PATTERNS = ((128, 1), (512, 4), (2048, 16))
DIL_SPAN = 128
MOBA_BLOCK = 256
MOBA_TOPK = 3
MOBA_QBLOCKS = 2
NUM_BUCKETS = 32
MAX_EXACT = 16
REL_MAX_DISTANCE = 2048
ROPE_THETA = 10000.0
EPS = 1e-6
NEG = -1e30
LOG2E = math.log2(math.e)
ONES_ROWS = 16
VMEM_LIMIT = 56 * 1024 * 1024
IN_TILE = 1024
MLA_TQ = 1024
MLA_TK = 512

C_QA, C_KA, C_VA = 0, 384, 768
C_CQ, C_CKV, C_KR = 1152, 1536, 1664
C_QC, C_KC, C_VC = 1792, 2048, 2304
D_IN_PAD = 2560

BF16 = jnp.bfloat16
F32 = jnp.float32


def _bucket(dist):
    n = jnp.maximum(dist, 0)
    nf = jnp.maximum(n, 1).astype(F32)
    large = MAX_EXACT + (jnp.log(nf / MAX_EXACT) / math.log(REL_MAX_DISTANCE / MAX_EXACT)
                         * (NUM_BUCKETS - MAX_EXACT)).astype(jnp.int32)
    large = jnp.minimum(large, NUM_BUCKETS - 1)
    return jnp.where(n < MAX_EXACT, n, large)


def _rms(x, g):
    return x * lax.rsqrt(jnp.mean(x * x, axis=-1, keepdims=True) + EPS) * g


def _dot(a, b):
    return jnp.dot(a, b, preferred_element_type=F32)


def _dot_nt(a, b, precision=None):
    return lax.dot_general(a, b, (((1,), (1,)), ((), ())), preferred_element_type=F32,
                           precision=precision)


def _rope(x, cos, sin_a, sin_b):
    return x * cos + pltpu.roll(x, LANES - 16, 1) * sin_a + pltpu.roll(x, 16, 1) * sin_b


def _in_kernel(x_ref, g_ref, w_ref, gq_ref, gkv_ref, wuqT_ref, wk_ref, wvT_ref, wqcT_ref, wvcT_ref,
               cos_ref, sa_ref, sb_ref, cosT_ref, sinT_ref,
               qa_ref, ka_ref, va_ref, qmT_ref, km_ref, vmT_ref, qcT_ref, kc_ref, vcT_ref, kmean_ref):
    hb = _rms(x_ref[...], g_ref[...]).astype(BF16)

    def proj(c0, n):
        return _dot(hb, w_ref[:, c0:c0 + n])

    qa_ref[...] = proj(C_QA, D_DIL) * (HEAD_DIM ** -0.5 * LOG2E)
    ka_ref[...] = proj(C_KA, D_DIL)
    va_ref[...] = proj(C_VA, D_DIL)

    cq = _rms(proj(C_CQ, MLA_Q_LORA), gq_ref[...]).astype(BF16)
    qT = _dot_nt(wuqT_ref[...], cq)
    cosT, sinT = cosT_ref[...], sinT_ref[...]
    scale = (MLA_NOPE + MLA_ROPE) ** -0.5 * LOG2E
    half = MLA_ROPE // 2
    for h in range(H_MLA):
        r0 = h * LANES
        x1 = qT[r0 + MLA_NOPE:r0 + MLA_NOPE + half]
        x2 = qT[r0 + MLA_NOPE + half:r0 + MLA_NOPE + MLA_ROPE]
        roped = jnp.concatenate([qT[r0:r0 + MLA_NOPE], x1 * cosT - x2 * sinT, x2 * cosT + x1 * sinT,
                                 qT[r0 + MLA_NOPE + MLA_ROPE:r0 + LANES]], axis=0)
        qmT_ref[r0:r0 + LANES, :] = (roped * scale).astype(BF16)
    ckv = _rms(proj(C_CKV, MLA_KV_LORA), gkv_ref[...]).astype(BF16)
    kn = _dot(ckv, wk_ref[...])
    vmT = _dot_nt(wvT_ref[...], ckv).astype(BF16)
    for i in range(vmT_ref.shape[0]):
        vmT_ref[i] = vmT[:, i * MLA_TK:(i + 1) * MLA_TK]
    kr = _rope(proj(C_KR, LANES), cos_ref[...], sa_ref[...], sb_ref[...])
    for h in range(H_MLA):
        sl = slice(h * LANES, (h + 1) * LANES)
        km_ref[:, sl] = (kn[:, sl] + kr).astype(BF16)

    qcT_ref[...] = _dot_nt(wqcT_ref[...], hb) * (HEAD_DIM ** -0.5 * LOG2E)
    kc = proj(C_KC, D_MOBA)
    kc_ref[...] = kc.astype(BF16)
    vcT = _dot_nt(wvcT_ref[...], hb).astype(BF16)
    for i in range(kc.shape[0] // MOBA_BLOCK):
        sl = slice(i * MOBA_BLOCK, (i + 1) * MOBA_BLOCK)
        vcT_ref[i] = vcT[:, sl]
        kmean_ref[i] = jnp.mean(kc[sl], axis=0, keepdims=True)


def _in_proj(x, layer, g, w, gq, gkv, wuqT, wk, wvT, wqcT, wvcT, cos, sin_a, sin_b, cosT, sinT, seq, tm):
    t, d = x.shape
    nt = seq // tm
    row = lambda i: (i, 0)
    col = lambda i: (0, i)
    lead = lambda i: (i, 0, 0)
    pos = lambda i: (i % nt, 0)
    posT = lambda i: (0, i % nt)
    full = lambda a: pl.BlockSpec((None,) + a.shape[1:], lambda i: (layer, 0, 0), pipeline_mode=pl.Buffered(1))
    nb = tm // MOBA_BLOCK
    outs = [
        ((t, D_DIL), F32, (tm, D_DIL), row), ((t, D_DIL), F32, (tm, D_DIL), row),
        ((t, D_DIL), F32, (tm, D_DIL), row),
        ((H_MLA * LANES, t), BF16, (H_MLA * LANES, tm), col),
        ((t, H_MLA * LANES), BF16, (tm, H_MLA * LANES), row),
        ((t // MLA_TK, D_MLA, MLA_TK), BF16, (tm // MLA_TK, D_MLA, MLA_TK), lead),
        ((D_MOBA, t), F32, (D_MOBA, tm), col),
        ((t, D_MOBA), BF16, (tm, D_MOBA), row),
        ((t // MOBA_BLOCK, D_MOBA, MOBA_BLOCK), BF16, (nb, D_MOBA, MOBA_BLOCK), lead),
        ((t // MOBA_BLOCK, 1, D_MOBA), F32, (nb, 1, D_MOBA), lead),
    ]
    return pl.pallas_call(
        _in_kernel,
        grid=(t // tm,),
        in_specs=[pl.BlockSpec((tm, d), row), full(g), full(w), full(gq), full(gkv), full(wuqT), full(wk),
                  full(wvT), full(wqcT), full(wvcT),
                  pl.BlockSpec((tm, LANES), pos), pl.BlockSpec((tm, LANES), pos), pl.BlockSpec((tm, LANES), pos),
                  pl.BlockSpec((MLA_ROPE // 2, tm), posT), pl.BlockSpec((MLA_ROPE // 2, tm), posT)],
        out_specs=[pl.BlockSpec(blk, imap) for _, _, blk, imap in outs],
        out_shape=[jax.ShapeDtypeStruct(shape, dt) for shape, dt, _, _ in outs],
        compiler_params=pltpu.CompilerParams(dimension_semantics=("parallel",),
                                             vmem_limit_bytes=VMEM_LIMIT),
        name="in_proj",
    )(x, g, w, gq, gkv, wuqT, wk, wvT, wqcT, wvcT, cos, sin_a, sin_b, cosT, sinT)


def _dil_kernel(q_ref, k_ref, v_ref, bias_ref, o_ref, oscr, lscr, *, unroll):
    seq = q_ref.shape[0]
    span = DIL_SPAN
    lane = lax.broadcasted_iota(jnp.int32, (span, LANES), 1)
    head0 = lane < HEAD_DIM

    for pi, (_, dil) in enumerate(DILATED_PATTERNS):
        nblk = seq // (span * dil)

        def rows(start, dil=dil):
            return pl.ds(start, span) if dil == 1 else pl.ds(start, span, stride=dil)

        def unit(u, carry, first, pi=pi, dil=dil, rows=rows):
            r = u if first else u % dil
            cur = r if first else (u // dil) * (span * dil) + r
            q = q_ref[rows(cur), :]
            qs = jnp.concatenate([jnp.where(head0, q, 0.0), jnp.where(head0, 0.0, q)], axis=0).astype(BF16)
            if first:
                kk = k_ref[rows(cur), :].astype(BF16)
                vv = v_ref[rows(cur), :].astype(BF16)
                s = _dot_nt(qs, kk) + bias_ref[pi, :, span:]
            else:
                prev = cur - span * dil
                kk = jnp.concatenate([k_ref[rows(prev), :], k_ref[rows(cur), :]], axis=0).astype(BF16)
                vv = jnp.concatenate([v_ref[rows(prev), :], v_ref[rows(cur), :]], axis=0).astype(BF16)
                s = _dot_nt(qs, kk) + bias_ref[pi]
            m = jnp.max(s, axis=-1, keepdims=True)
            p = jnp.exp2(s - m).astype(BF16)
            on = _dot(p, jnp.concatenate([vv, jnp.ones_like(vv)], axis=1))
            l = on[:, LANES:]
            o = on[:, :LANES] / l
            lse = m + jnp.log(l) * LOG2E
            oscr[pi, rows(cur), :] = jnp.where(head0, o[:span], o[span:])
            lscr[pi, rows(cur), :] = jnp.where(head0, lse[:span], lse[span:])
            return carry

        lax.fori_loop(0, dil, functools.partial(unit, first=True), 0, unroll=min(unroll, dil))
        lax.fori_loop(dil, nblk * dil, functools.partial(unit, first=False), 0, unroll=unroll)

    ct = 512
    for c in range(seq // ct):
        sl = pl.ds(c * ct, ct)
        l0, l1, l2 = lscr[0, sl, :], lscr[1, sl, :], lscr[2, sl, :]
        mx = jnp.maximum(jnp.maximum(l0, l1), l2)
        e0, e1, e2 = jnp.exp2(l0 - mx), jnp.exp2(l1 - mx), jnp.exp2(l2 - mx)
        num = e0 * oscr[0, sl, :] + e1 * oscr[1, sl, :] + e2 * oscr[2, sl, :]
        o_ref[sl, :] = num / (e0 + e1 + e2)


def _dilated(qa, ka, va, bias, batch, seq, unroll=8):
    t = qa.shape[0]
    npair = H_DIL // 2
    blk = pl.BlockSpec((seq, LANES), lambda b, p: (b, p))
    return pl.pallas_call(
        functools.partial(_dil_kernel, unroll=unroll),
        grid=(batch, npair),
        in_specs=[blk, blk, blk,
                  pl.BlockSpec((None, len(DILATED_PATTERNS), 2 * DIL_SPAN, 2 * DIL_SPAN),
                               lambda b, p: (p, 0, 0, 0))],
        out_specs=blk,
        out_shape=jax.ShapeDtypeStruct((t, D_DIL), F32),
        scratch_shapes=[pltpu.VMEM((len(DILATED_PATTERNS), seq, LANES), F32),
                        pltpu.VMEM((len(DILATED_PATTERNS), seq, LANES), F32)],
        compiler_params=pltpu.CompilerParams(dimension_semantics=("parallel", "parallel"),
                                             vmem_limit_bytes=VMEM_LIMIT),
        name="dilated",
    )(qa, ka, va, bias)


def _mla_kernel(qT_ref, k_ref, vT_ref, o_ref, *, tq, tk, nsub):
    qt = tq // tk
    first = pl.program_id(2) * qt
    sub = tk // nsub
    zero = jnp.zeros((LANES, tq), BF16)
    rhs = jnp.concatenate([jnp.concatenate([qT_ref[:LANES, :], zero], axis=1),
                           jnp.concatenate([zero, qT_ref[LANES:, :]], axis=1)], axis=0)
    ones = jnp.ones((ONES_ROWS, sub), BF16)

    def both(x, lo):
        return x if lo == 0 else jnp.concatenate([x[:, lo:tq], x[:, tq + lo:]], axis=1)

    def scores(j, u, lo, diag):
        w = tq - lo
        s = _dot(k_ref[pl.ds(pl.multiple_of(j * tk + u * sub, sub), sub), :], both(rhs, lo))
        if diag:
            krow = lax.broadcasted_iota(jnp.int32, (sub, 2 * w), 0)
            qcol = lax.broadcasted_iota(jnp.int32, (sub, 2 * w), 1)
            qcol = jnp.where(qcol >= w, qcol - w, qcol)
            s = jnp.where(krow <= qcol, s, NEG)
        return s

    def update(j, u, s, m, acc):
        w = s.shape[1] // 2
        m_new = jnp.maximum(m, jnp.max(s, axis=0, keepdims=True))
        p = jnp.exp2(s - m_new).astype(BF16)
        vu = vT_ref[j][:, u * sub:(u + 1) * sub]
        pv = jnp.concatenate([_dot(jnp.concatenate([vu[:MLA_V], ones], axis=0), p[:, :w]),
                              _dot(jnp.concatenate([vu[MLA_V:], ones], axis=0), p[:, w:])], axis=1)
        return m_new, jnp.exp2(m - m_new) * acc + pv

    def past(tiles, carry):
        ss = [(j, u, scores(j, u, 0, False)) for j in tiles for u in range(nsub)]
        for j, u, s in ss:
            carry = update(j, u, s, *carry)
        return carry

    init = (jnp.full((1, 2 * tq), NEG, F32), jnp.zeros((MLA_V + ONES_ROWS, 2 * tq), F32))
    carry = lax.fori_loop(0, first // 2, lambda t, c: past((2 * t, 2 * t + 1), c), init)
    m, acc = lax.fori_loop(2 * (first // 2), first, lambda j, c: past((j,), c), carry)
    groups = [(first + g // nsub, g % nsub, g * sub) for g in range(qt * nsub)]
    ss = [scores(j, u, lo, True) for j, u, lo in groups]
    for (j, u, lo), s in zip(groups, ss):
        m_u, acc_u = update(j, u, s, both(m, lo), both(acc, lo))
        if lo:
            m = jnp.concatenate([m[:, :lo], m_u[:, :tq - lo], m[:, tq:tq + lo], m_u[:, tq - lo:]], axis=1)
            acc = jnp.concatenate([acc[:, :lo], acc_u[:, :tq - lo], acc[:, tq:tq + lo], acc_u[:, tq - lo:]], axis=1)
        else:
            m, acc = m_u, acc_u
    oT = acc[:MLA_V] / acc[MLA_V:MLA_V + 1]
    o_ref[...] = jnp.concatenate([oT[:, :tq], oT[:, tq:]], axis=0).T


def _mla(qmT, km, vmT, batch, seq, tq, tk, nsub=2):
    t = km.shape[0]
    nq = seq // tq
    npair = H_MLA // 2
    assert tq % tk == 0 and seq % tq == 0
    return pl.pallas_call(
        functools.partial(_mla_kernel, tq=tq, tk=tk, nsub=nsub),
        grid=(batch, npair, nq),
        in_specs=[pl.BlockSpec((2 * LANES, tq), lambda b, p, i: (p, b * nq + i)),
                  pl.BlockSpec((seq, 2 * LANES), lambda b, p, i: (b, p)),
                  pl.BlockSpec((seq // tk, LANES, tk), lambda b, p, i: (b, p, 0))],
        out_specs=pl.BlockSpec((tq, LANES), lambda b, p, i: (b * nq + i, p)),
        out_shape=jax.ShapeDtypeStruct((t, D_MLA), F32),
        compiler_params=pltpu.CompilerParams(dimension_semantics=("parallel", "parallel", "arbitrary"),
                                             vmem_limit_bytes=VMEM_LIMIT),
        name="mla",
    )(qmT, km, vmT)


def _moba_kernel(qT_ref, k_ref, vT_ref, kmean_ref, bias_ref, o_ref, pen_ref, *, nfar):
    blk = MOBA_BLOCK
    qb = MOBA_QBLOCKS
    per_head = qb * blk
    wide = H_MOBA * per_head
    j0 = pl.program_id(1) * qb
    nblk = kmean_ref.shape[0]
    qT = qT_ref[...]
    rhead = lax.broadcasted_iota(jnp.int32, (D_MOBA, per_head), 0) // HEAD_DIM
    qsT = jnp.concatenate([jnp.where(rhead == h, qT, 0.0) for h in range(H_MOBA)], axis=1)

    gate = jnp.dot(kmean_ref[:, 0, :], qsT, preferred_element_type=F32, precision=lax.Precision.HIGHEST)
    bidx = lax.broadcasted_iota(jnp.int32, (nblk, wide), 0)
    own = j0 + (lax.broadcasted_iota(jnp.int32, (nblk, wide), 1) % per_head) // blk
    gate = jnp.where(bidx < own, gate, -jnp.inf)
    pen = jnp.full((nblk, wide), 2 * NEG, F32)
    for _ in range(MOBA_TOPK):
        mx = jnp.max(gate, axis=0, keepdims=True)
        first = jnp.min(jnp.where(gate == mx, bidx, nblk), axis=0, keepdims=True)
        hit = bidx == first
        pen = jnp.where(hit & (mx > -jnp.inf), 0.0, pen)
        gate = jnp.where(hit, -jnp.inf, gate)

    pen_ref[...] = jnp.where(bidx == own, 0.0, pen)
    rhs = qsT.astype(BF16)

    def scores(n):
        s = _dot(k_ref[pl.ds(pl.multiple_of(n * blk, blk), blk), :], rhs)
        tiles = [bias_ref[jnp.clip(j0 + c - n, 0, nfar)] for c in range(qb)]
        bias = jnp.concatenate([tiles[c][:, h * blk:(h + 1) * blk] for h in range(H_MOBA) for c in range(qb)],
                               axis=1)
        return s + bias

    ones = jnp.ones((ONES_ROWS, blk), BF16)

    def weighted(n, p):
        vT = vT_ref[n]
        return jnp.concatenate(
            [_dot(jnp.concatenate([vT[h * HEAD_DIM:(h + 1) * HEAD_DIM], ones], axis=0),
                  p[:, h * per_head:(h + 1) * per_head]) for h in range(H_MOBA)], axis=1)

    def pair(t, carry):
        m, acc = carry
        n0 = j0 + qb - 1 - 2 * t
        n1 = n0 - 1
        s0, s1 = scores(n0), scores(n1)
        for n, s in ((n0, s0), (n1, s1)):
            pen = pen_ref[pl.ds(n, 1), :]
            m_new = jnp.maximum(m, jnp.max(s, axis=0, keepdims=True) + pen)
            p = jnp.exp2(s - (m_new - pen)).astype(BF16)
            acc = jnp.exp2(m - m_new) * acc + weighted(n, p)
            m = m_new
        return m, acc

    init = (jnp.full((1, wide), NEG, F32), jnp.zeros((HEAD_DIM + ONES_ROWS, wide), F32))
    _, acc = lax.fori_loop(0, (j0 + qb) // 2, pair, init)
    oT = acc[:HEAD_DIM] / acc[HEAD_DIM:HEAD_DIM + 1]
    o_ref[...] = jnp.concatenate([oT[:, h * per_head:(h + 1) * per_head] for h in range(H_MOBA)], axis=0).T


def _moba(qcT, kc, vcT, kmean, bias, batch, seq):
    t = kc.shape[0]
    nblk = seq // MOBA_BLOCK
    nstep = nblk // MOBA_QBLOCKS
    rows = MOBA_QBLOCKS * MOBA_BLOCK
    nfar = bias.shape[0] - 1
    assert MOBA_QBLOCKS % 2 == 0 and nblk % MOBA_QBLOCKS == 0
    return pl.pallas_call(
        functools.partial(_moba_kernel, nfar=nfar),
        grid=(batch, nstep),
        in_specs=[pl.BlockSpec((D_MOBA, rows), lambda b, j: (0, b * nstep + j)),
                  pl.BlockSpec((seq, D_MOBA), lambda b, j: (b, 0)),
                  pl.BlockSpec((nblk, D_MOBA, MOBA_BLOCK), lambda b, j: (b, 0, 0)),
                  pl.BlockSpec((nblk, 1, D_MOBA), lambda b, j: (b, 0, 0)),
                  pl.BlockSpec(bias.shape, lambda b, j: (0, 0, 0), pipeline_mode=pl.Buffered(1))],
        out_specs=pl.BlockSpec((rows, D_MOBA), lambda b, j: (b * nstep + j, 0)),
        out_shape=jax.ShapeDtypeStruct((t, D_MOBA), F32),
        scratch_shapes=[pltpu.VMEM((nblk, H_MOBA * rows), F32)],
        compiler_params=pltpu.CompilerParams(dimension_semantics=("parallel", "arbitrary"),
                                             vmem_limit_bytes=VMEM_LIMIT),
        name="moba",
    )(qcT, kc, vcT, kmean, bias)


def _post_kernel(x_ref, oa_ref, ob_ref, oc_ref, gmix_ref, wo_ref, gmlp_ref, wup_ref, wdn_ref, gfin_ref,
                 y_ref, *, final, ff_chunk):
    gm = gmix_ref[...]
    x = x_ref[...]
    lo = 0
    for o_ref in (oa_ref, ob_ref, oc_ref):
        n = o_ref.shape[1]
        mixed = _rms(o_ref[...], gm[:, lo:lo + n]).astype(BF16)
        x = x + _dot(mixed, wo_ref[lo:lo + n, :])
        lo += n
    hb = _rms(x, gmlp_ref[...]).astype(BF16)
    y_ref[...] = x
    for c in range(wup_ref.shape[1] // ff_chunk):
        cs = slice(c * ff_chunk, (c + 1) * ff_chunk)
        u = jnp.maximum(_dot(hb, wup_ref[:, cs]), 0.0)
        y_ref[...] += _dot((u * u).astype(BF16), wdn_ref[cs, :])
    if final:
        y_ref[...] = _rms(y_ref[...], gfin_ref[...])


def _post(x, oa, ob, oc, layer, gmix, wo, gmlp, wup, wdn, gfin, final, tm=512, ff_chunk=1024):
    t, d = x.shape
    row = lambda i: (i, 0)
    stacked = lambda a: pl.BlockSpec((None,) + a.shape[1:], lambda i: (layer, 0, 0), pipeline_mode=pl.Buffered(1))
    return pl.pallas_call(
        functools.partial(_post_kernel, final=final, ff_chunk=ff_chunk),
        grid=(t // tm,),
        in_specs=[pl.BlockSpec((tm, d), row), pl.BlockSpec((tm, oa.shape[1]), row),
                  pl.BlockSpec((tm, ob.shape[1]), row), pl.BlockSpec((tm, oc.shape[1]), row),
                  stacked(gmix), stacked(wo), stacked(gmlp), stacked(wup), stacked(wdn),
                  pl.BlockSpec(gfin.shape, lambda i: (0, 0))],
        out_specs=pl.BlockSpec((tm, d), row),
        out_shape=jax.ShapeDtypeStruct((t, d), F32),
        compiler_params=pltpu.CompilerParams(dimension_semantics=("parallel",),
                                             vmem_limit_bytes=VMEM_LIMIT),
        name="post",
    )(x, oa, ob, oc, gmix, wo, gmlp, wup, wdn, gfin)


def _rope_tables(seq):
    inv_freq = ROPE_THETA ** (-jnp.arange(0, MLA_ROPE, 2, dtype=F32) / MLA_ROPE)
    ang = jnp.arange(seq, dtype=F32)[:, None] * inv_freq[None, :]
    cos, sin = jnp.cos(ang), jnp.sin(ang)
    half = MLA_ROPE // 2
    one = jnp.ones((seq, MLA_NOPE), F32)
    zero = jnp.zeros((seq, MLA_NOPE), F32)
    zh = jnp.zeros((seq, half), F32)
    tail1 = jnp.ones((seq, LANES - MLA_NOPE - MLA_ROPE), F32)
    tail0 = jnp.zeros((seq, LANES - MLA_NOPE - MLA_ROPE), F32)
    cos_t = jnp.concatenate([one, cos, cos, tail1], axis=1)
    sin_a = jnp.concatenate([zero, -sin, zh, tail0], axis=1)
    sin_b = jnp.concatenate([zero, zh, sin, tail0], axis=1)
    return cos_t, sin_a, sin_b, cos.T, sin.T


def _lookup(tab, bucket):
    col = lambda b: tab[b][(slice(None),) + (None,) * bucket.ndim]
    out = jnp.broadcast_to(col(0), (tab.shape[1],) + bucket.shape)
    for b in range(1, NUM_BUCKETS):
        out = jnp.where(bucket[None] == b, col(b), out)
    return out.astype(F32)


def _toeplitz(v, rows, cols):
    lead = v.shape[:-1]
    period = rows + cols
    u = jnp.pad(v[..., ::-1], [(0, 0)] * len(lead) + [(0, 1)])
    flat = jnp.tile(u, (1,) * len(lead) + (rows,))[..., :rows * (period - 1)]
    return flat.reshape(lead + (rows, period - 1))[..., rows - 1:rows - 1 + cols]


def _dil_bias(bias_tab):
    span = DIL_SPAN
    npat = len(DILATED_PATTERNS)
    diff = jnp.arange(3 * span - 1) - (span - 1)
    in_band = (diff >= 0) & (diff <= span)
    tabs = []
    for _, dil in DILATED_PATTERNS:
        by_diff = jnp.where(in_band[None], _lookup(bias_tab, _bucket(diff * dil)) * LOG2E, NEG)
        tabs.append(_toeplitz(by_diff, span, 2 * span))
    b = jnp.stack(tabs, axis=1)
    b = b.reshape(H_DIL // 2, 2, npat, span, 2 * span).transpose(0, 2, 1, 3, 4)
    return b.reshape(H_DIL // 2, npat, 2 * span, 2 * span)


def _moba_bias(bias_tab, nblk):
    blk = MOBA_BLOCK
    nfar = min(nblk - 1, REL_MAX_DISTANCE // blk + 1)
    dist = jnp.arange((nfar + 2) * blk - 1) - (blk - 1)
    by_dist = jnp.where(dist[None] >= 0, _lookup(bias_tab, _bucket(dist)) * LOG2E, NEG)
    windows = jnp.stack([by_dist[:, d * blk:(d + 2) * blk - 1] for d in range(nfar + 1)], axis=1)
    b = _toeplitz(windows, blk, blk)
    return b.transpose(1, 3, 0, 2).reshape(nfar + 1, blk, H_MOBA * blk)


def _pad_w_in(w):
    sizes = (D_DIL, D_DIL, D_DIL, MLA_Q_LORA, MLA_KV_LORA, MLA_ROPE, D_MOBA, D_MOBA, D_MOBA)
    parts, lo = [], 0
    for n in sizes:
        parts.append(w[..., lo:lo + n])
        lo += n
    zeros = lambda n: jnp.zeros(w.shape[:-1] + (n,), w.dtype)
    parts[5] = jnp.concatenate([zeros(MLA_NOPE), parts[5], zeros(LANES - MLA_NOPE - MLA_ROPE)], axis=-1)
    return jnp.concatenate(parts, axis=-1).astype(BF16)


def _pad_w_uq_t(w):
    depth, r, _ = w.shape
    w = w.reshape(depth, r, H_MLA, MLA_NOPE + MLA_ROPE)
    w = jnp.pad(w, ((0, 0), (0, 0), (0, 0), (0, LANES - MLA_NOPE - MLA_ROPE)))
    return w.reshape(depth, r, H_MLA * LANES).transpose(0, 2, 1).astype(BF16)


def _split_w_ukv(w):
    depth, r, _ = w.shape
    w = w.reshape(depth, r, H_MLA, MLA_NOPE + MLA_V)
    wk = jnp.pad(w[..., :MLA_NOPE], ((0, 0), (0, 0), (0, 0), (0, LANES - MLA_NOPE)))
    wv = w[..., MLA_NOPE:].reshape(depth, r, H_MLA * MLA_V)
    return wk.reshape(depth, r, H_MLA * LANES).astype(BF16), wv.transpose(0, 2, 1).astype(BF16)


def kernel(x, g_attn, w_in, g_q_lora, g_kv_lora, w_uq, w_ukv, rel_bias, g_mix, w_o, g_mlp, w_up, w_down,
           g_final):
    batch, seq, d = x.shape
    depth = w_in.shape[0]
    assert seq % (DIL_SPAN * max(dil for _, dil in DILATED_PATTERNS)) == 0
    assert seq % IN_TILE == 0 and IN_TILE % MLA_TK == 0
    cos_t, sin_a, sin_b, cosT, sinT = _rope_tables(seq)
    bias_dil = _dil_bias(rel_bias[:, :H_DIL])
    bias_moba = _moba_bias(rel_bias[:, H_DIL:], seq // MOBA_BLOCK)
    rows = lambda v: v[:, None, :].astype(F32)
    w = _pad_w_in(w_in)
    wk, wvT = _split_w_ukv(w_ukv)
    in_params = (rows(g_attn), w, rows(g_q_lora), rows(g_kv_lora), _pad_w_uq_t(w_uq), wk, wvT,
                 w[:, :, C_QC:C_QC + D_MOBA].transpose(0, 2, 1), w[:, :, C_VC:C_VC + D_MOBA].transpose(0, 2, 1))
    post_params = (rows(g_mix), w_o.astype(BF16), rows(g_mlp), w_up.astype(BF16), w_down.astype(BF16))
    gfin = g_final.reshape(1, -1).astype(F32)
    xf = x.reshape(batch * seq, d)
    for l in range(depth):
        qa, ka, va, qmT, km, vmT, qcT, kc, vcT, kmean = _in_proj(
            xf, l, *in_params, cos_t, sin_a, sin_b, cosT, sinT, seq, tm=IN_TILE)
        oa = _dilated(qa, ka, va, bias_dil, batch, seq)
        ob = _mla(qmT, km, vmT, batch, seq, tq=MLA_TQ, tk=MLA_TK)
        oc = _moba(qcT, kc, vcT, kmean, bias_moba, batch, seq)
        xf = _post(xf, oa, ob, oc, l, *post_params, gfin, final=(l == depth - 1))
    return xf.reshape(batch, seq, d)
```

```python
import functools
import math

import jax
import jax.numpy as jnp
from jax import lax
from jax.experimental import pallas as pl
from jax.experimental.pallas import tpu as pltpu

LANES = 128
HEAD_DIM = 64
H_DIL, H_MLA, H_MOBA = 6, 6, 4
D_DIL = H_DIL * HEAD_DIM
MLA_NOPE, MLA_ROPE, MLA_V = 64, 32, 64
MLA_Q_LORA, MLA_KV_LORA = 384, 128
D_MLA = H_MLA * MLA_V
D_MOBA = H_MOBA * HEAD_DIM
DILATED_PATTERNS = ((128, 1), (512, 4), (2048, 16))
DIL_SPAN = 128
MOBA_BLOCK = 256
MOBA_TOPK = 3
MOBA_QBLOCKS = 2
NUM_BUCKETS = 32
MAX_EXACT = 16
REL_MAX_DISTANCE = 2048
ROPE_THETA = 10000.0
EPS = 1e-6
NEG = -1e30
LOG2E = math.log2(math.e)
ONES_ROWS = 16
VMEM_LIMIT = 56 * 1024 * 1024
IN_TILE = 1024
MLA_TQ = 2048
MLA_TK = 512

C_QA, C_KA, C_VA = 0, 384, 768
C_CQ, C_CKV, C_KR = 1152, 1536, 1664
C_QC, C_KC, C_VC = 1792, 2048, 2304
D_IN_PAD = 2560

BF16 = jnp.bfloat16
F32 = jnp.float32


def _bucket(dist):
    n = jnp.maximum(dist, 0)
    nf = jnp.maximum(n, 1).astype(F32)
    large = MAX_EXACT + (jnp.log(nf / MAX_EXACT) / math.log(REL_MAX_DISTANCE / MAX_EXACT)
                         * (NUM_BUCKETS - MAX_EXACT)).astype(jnp.int32)
    large = jnp.minimum(large, NUM_BUCKETS - 1)
    return jnp.where(n < MAX_EXACT, n, large)


def _rms(x, g):
    return x * lax.rsqrt(jnp.mean(x * x, axis=-1, keepdims=True) + EPS) * g


def _dot(a, b):
    return jnp.dot(a, b, preferred_element_type=F32)


def _dot_nt(a, b, precision=None):
    return lax.dot_general(a, b, (((1,), (1,)), ((), ())), preferred_element_type=F32,
                           precision=precision)


def _rope(x, cos, sin_a, sin_b):
    return x * cos + pltpu.roll(x, LANES - 16, 1) * sin_a + pltpu.roll(x, 16, 1) * sin_b


def _in_kernel(x_ref, g_ref, w_ref, gq_ref, gkv_ref, wuqT_ref, wk_ref, wvT_ref, wqcT_ref, wvcT_ref,
               cos_ref, sa_ref, sb_ref, cosT_ref, sinT_ref,
               qa_ref, ka_ref, va_ref, qmT_ref, km_ref, vmT_ref, qcT_ref, kc_ref, vcT_ref, kmean_ref):
    hb = _rms(x_ref[...], g_ref[...]).astype(BF16)

    def proj(c0, n):
        return _dot(hb, w_ref[:, c0:c0 + n])

    qa_ref[...] = proj(C_QA, D_DIL) * (HEAD_DIM ** -0.5 * LOG2E)
    ka_ref[...] = proj(C_KA, D_DIL)
    va_ref[...] = proj(C_VA, D_DIL)

    cq = _rms(proj(C_CQ, MLA_Q_LORA), gq_ref[...]).astype(BF16)
    qT = _dot_nt(wuqT_ref[...], cq)
    cosT, sinT = cosT_ref[...], sinT_ref[...]
    scale = (MLA_NOPE + MLA_ROPE) ** -0.5 * LOG2E
    half = MLA_ROPE // 2
    for h in range(H_MLA):
        r0 = h * LANES
        x1 = qT[r0 + MLA_NOPE:r0 + MLA_NOPE + half]
        x2 = qT[r0 + MLA_NOPE + half:r0 + MLA_NOPE + MLA_ROPE]
        roped = jnp.concatenate([qT[r0:r0 + MLA_NOPE], x1 * cosT - x2 * sinT, x2 * cosT + x1 * sinT,
                                 qT[r0 + MLA_NOPE + MLA_ROPE:r0 + LANES]], axis=0)
        qmT_ref[r0:r0 + LANES, :] = (roped * scale).astype(BF16)
    ckv = _rms(proj(C_CKV, MLA_KV_LORA), gkv_ref[...]).astype(BF16)
    kn = _dot(ckv, wk_ref[...])
    vmT = _dot_nt(wvT_ref[...], ckv).astype(BF16)
    for i in range(vmT_ref.shape[0]):
        vmT_ref[i] = vmT[:, i * MLA_TK:(i + 1) * MLA_TK]
    kr = _rope(proj(C_KR, LANES), cos_ref[...], sa_ref[...], sb_ref[...])
    for h in range(H_MLA):
        sl = slice(h * LANES, (h + 1) * LANES)
        km_ref[:, sl] = (kn[:, sl] + kr).astype(BF16)

    qcT_ref[...] = _dot_nt(wqcT_ref[...], hb) * (HEAD_DIM ** -0.5 * LOG2E)
    kc = proj(C_KC, D_MOBA)
    kc_ref[...] = kc.astype(BF16)
    vcT = _dot_nt(wvcT_ref[...], hb).astype(BF16)
    for i in range(kc.shape[0] // MOBA_BLOCK):
        sl = slice(i * MOBA_BLOCK, (i + 1) * MOBA_BLOCK)
        vcT_ref[i] = vcT[:, sl]
        kmean_ref[i] = jnp.mean(kc[sl], axis=0, keepdims=True)


def _in_proj(x, layer, g, w, gq, gkv, wuqT, wk, wvT, wqcT, wvcT, cos, sin_a, sin_b, cosT, sinT, seq, tm):
    t, d = x.shape
    nt = seq // tm
    row = lambda i: (i, 0)
    col = lambda i: (0, i)
    lead = lambda i: (i, 0, 0)
    pos = lambda i: (i % nt, 0)
    posT = lambda i: (0, i % nt)
    full = lambda a: pl.BlockSpec((None,) + a.shape[1:], lambda i: (layer, 0, 0), pipeline_mode=pl.Buffered(1))
    nb = tm // MOBA_BLOCK
    outs = [
        ((t, D_DIL), F32, (tm, D_DIL), row), ((t, D_DIL), F32, (tm, D_DIL), row),
        ((t, D_DIL), F32, (tm, D_DIL), row),
        ((H_MLA * LANES, t), BF16, (H_MLA * LANES, tm), col),
        ((t, H_MLA * LANES), BF16, (tm, H_MLA * LANES), row),
        ((t // MLA_TK, D_MLA, MLA_TK), BF16, (tm // MLA_TK, D_MLA, MLA_TK), lead),
        ((D_MOBA, t), F32, (D_MOBA, tm), col),
        ((t, D_MOBA), BF16, (tm, D_MOBA), row),
        ((t // MOBA_BLOCK, D_MOBA, MOBA_BLOCK), BF16, (nb, D_MOBA, MOBA_BLOCK), lead),
        ((t // MOBA_BLOCK, 1, D_MOBA), F32, (nb, 1, D_MOBA), lead),
    ]
    return pl.pallas_call(
        _in_kernel,
        grid=(t // tm,),
        in_specs=[pl.BlockSpec((tm, d), row), full(g), full(w), full(gq), full(gkv), full(wuqT), full(wk),
                  full(wvT), full(wqcT), full(wvcT),
                  pl.BlockSpec((tm, LANES), pos), pl.BlockSpec((tm, LANES), pos), pl.BlockSpec((tm, LANES), pos),
                  pl.BlockSpec((MLA_ROPE // 2, tm), posT), pl.BlockSpec((MLA_ROPE // 2, tm), posT)],
        out_specs=[pl.BlockSpec(blk, imap) for _, _, blk, imap in outs],
        out_shape=[jax.ShapeDtypeStruct(shape, dt) for shape, dt, _, _ in outs],
        compiler_params=pltpu.CompilerParams(dimension_semantics=("parallel",),
                                             vmem_limit_bytes=VMEM_LIMIT),
        name="in_proj",
    )(x, g, w, gq, gkv, wuqT, wk, wvT, wqcT, wvcT, cos, sin_a, sin_b, cosT, sinT)


def _dil_kernel(q_ref, k_ref, v_ref, bias_ref, o_ref, oscr, lscr, *, unroll):
    seq = q_ref.shape[0]
    span = DIL_SPAN
    lane = lax.broadcasted_iota(jnp.int32, (span, LANES), 1)
    head0 = lane < HEAD_DIM

    for pi, (_, dil) in enumerate(DILATED_PATTERNS):
        nblk = seq // (span * dil)

        def rows(start, dil=dil):
            return pl.ds(start, span) if dil == 1 else pl.ds(start, span, stride=dil)

        def unit(u, carry, first, pi=pi, dil=dil, rows=rows):
            r = u if first else u % dil
            cur = r if first else (u // dil) * (span * dil) + r
            q = q_ref[rows(cur), :]
            qs = jnp.concatenate([jnp.where(head0, q, 0.0), jnp.where(head0, 0.0, q)], axis=0).astype(BF16)
            if first:
                kk = k_ref[rows(cur), :].astype(BF16)
                vv = v_ref[rows(cur), :].astype(BF16)
                s = _dot_nt(qs, kk) + bias_ref[pi, :, span:]
            else:
                prev = cur - span * dil
                kk = jnp.concatenate([k_ref[rows(prev), :], k_ref[rows(cur), :]], axis=0).astype(BF16)
                vv = jnp.concatenate([v_ref[rows(prev), :], v_ref[rows(cur), :]], axis=0).astype(BF16)
                s = _dot_nt(qs, kk) + bias_ref[pi]
            m = jnp.max(s, axis=-1, keepdims=True)
            p = jnp.exp2(s - m).astype(BF16)
            on = _dot(p, jnp.concatenate([vv, jnp.ones_like(vv)], axis=1))
            l = on[:, LANES:]
            o = on[:, :LANES] / l
            lse = m + jnp.log(l) * LOG2E
            oscr[pi, rows(cur), :] = jnp.where(head0, o[:span], o[span:])
            lscr[pi, rows(cur), :] = jnp.where(head0, lse[:span], lse[span:])
            return carry

        lax.fori_loop(0, dil, functools.partial(unit, first=True), 0, unroll=min(unroll, dil))
        lax.fori_loop(dil, nblk * dil, functools.partial(unit, first=False), 0, unroll=unroll)

    ct = 512
    for c in range(seq // ct):
        sl = pl.ds(c * ct, ct)
        l0, l1, l2 = lscr[0, sl, :], lscr[1, sl, :], lscr[2, sl, :]
        mx = jnp.maximum(jnp.maximum(l0, l1), l2)
        e0, e1, e2 = jnp.exp2(l0 - mx), jnp.exp2(l1 - mx), jnp.exp2(l2 - mx)
        num = e0 * oscr[0, sl, :] + e1 * oscr[1, sl, :] + e2 * oscr[2, sl, :]
        o_ref[sl, :] = num / (e0 + e1 + e2)


def _dilated(qa, ka, va, bias, batch, seq, unroll=8):
    t = qa.shape[0]
    npair = H_DIL // 2
    blk = pl.BlockSpec((seq, LANES), lambda b, p: (b, p))
    return pl.pallas_call(
        functools.partial(_dil_kernel, unroll=unroll),
        grid=(batch, npair),
        in_specs=[blk, blk, blk,
                  pl.BlockSpec((None, len(DILATED_PATTERNS), 2 * DIL_SPAN, 2 * DIL_SPAN),
                               lambda b, p: (p, 0, 0, 0))],
        out_specs=blk,
        out_shape=jax.ShapeDtypeStruct((t, D_DIL), F32),
        scratch_shapes=[pltpu.VMEM((len(DILATED_PATTERNS), seq, LANES), F32),
                        pltpu.VMEM((len(DILATED_PATTERNS), seq, LANES), F32)],
        compiler_params=pltpu.CompilerParams(dimension_semantics=("parallel", "parallel"),
                                             vmem_limit_bytes=VMEM_LIMIT),
        name="dilated",
    )(qa, ka, va, bias)


def _mla_kernel(qT_ref, k_ref, vT_ref, o_ref, *, tq, tk, nsub):
    qt = tq // tk
    first = pl.program_id(2) * qt
    sub = tk // nsub
    zero = jnp.zeros((LANES, tq), BF16)
    rhs = jnp.concatenate([jnp.concatenate([qT_ref[:LANES, :], zero], axis=1),
                           jnp.concatenate([zero, qT_ref[LANES:, :]], axis=1)], axis=0)
    ones = jnp.ones((ONES_ROWS, sub), BF16)

    def both(x, lo):
        return x if lo == 0 else jnp.concatenate([x[:, lo:tq], x[:, tq + lo:]], axis=1)

    def scores(j, u, lo, diag):
        w = tq - lo
        s = _dot(k_ref[pl.ds(pl.multiple_of(j * tk + u * sub, sub), sub), :], both(rhs, lo))
        if diag:
            krow = lax.broadcasted_iota(jnp.int32, (sub, 2 * w), 0)
            qcol = lax.broadcasted_iota(jnp.int32, (sub, 2 * w), 1)
            qcol = jnp.where(qcol >= w, qcol - w, qcol)
            s = jnp.where(krow <= qcol, s, NEG)
        return s

    def update(j, u, s, m, acc):
        w = s.shape[1] // 2
        m_new = jnp.maximum(m, jnp.max(s, axis=0, keepdims=True))
        p = jnp.exp2(s - m_new).astype(BF16)
        vu = vT_ref[j][:, u * sub:(u + 1) * sub]
        pv = jnp.concatenate([_dot(jnp.concatenate([vu[:MLA_V], ones], axis=0), p[:, :w]),
                              _dot(jnp.concatenate([vu[MLA_V:], ones], axis=0), p[:, w:])], axis=1)
        return m_new, jnp.exp2(m - m_new) * acc + pv

    def past(tiles, carry):
        ss = [(j, u, scores(j, u, 0, False)) for j in tiles for u in range(nsub)]
        for j, u, s in ss:
            carry = update(j, u, s, *carry)
        return carry

    init = (jnp.full((1, 2 * tq), NEG, F32), jnp.zeros((MLA_V + ONES_ROWS, 2 * tq), F32))
    carry = lax.fori_loop(0, first // 2, lambda t, c: past((2 * t, 2 * t + 1), c), init)
    m, acc = lax.fori_loop(2 * (first // 2), first, lambda j, c: past((j,), c), carry)
    groups = [(first + g // nsub, g % nsub, g * sub) for g in range(qt * nsub)]
    ss = [scores(j, u, lo, True) for j, u, lo in groups]
    for (j, u, lo), s in zip(groups, ss):
        m_u, acc_u = update(j, u, s, both(m, lo), both(acc, lo))
        if lo:
            m = jnp.concatenate([m[:, :lo], m_u[:, :tq - lo], m[:, tq:tq + lo], m_u[:, tq - lo:]], axis=1)
            acc = jnp.concatenate([acc[:, :lo], acc_u[:, :tq - lo], acc[:, tq:tq + lo], acc_u[:, tq - lo:]], axis=1)
        else:
            m, acc = m_u, acc_u
    oT = acc[:MLA_V] / acc[MLA_V:MLA_V + 1]
    o_ref[...] = jnp.concatenate([oT[:, :tq], oT[:, tq:]], axis=0).T


def _mla(qmT, km, vmT, batch, seq, tq, tk, nsub=2):
    t = km.shape[0]
    nq = seq // tq
    npair = H_MLA // 2
    assert tq % tk == 0 and seq % tq == 0
    return pl.pallas_call(
        functools.partial(_mla_kernel, tq=tq, tk=tk, nsub=nsub),
        grid=(batch, npair, nq),
        in_specs=[pl.BlockSpec((2 * LANES, tq), lambda b, p, i: (p, b * nq + i)),
                  pl.BlockSpec((seq, 2 * LANES), lambda b, p, i: (b, p)),
                  pl.BlockSpec((seq // tk, LANES, tk), lambda b, p, i: (b, p, 0))],
        out_specs=pl.BlockSpec((tq, LANES), lambda b, p, i: (b * nq + i, p)),
        out_shape=jax.ShapeDtypeStruct((t, D_MLA), F32),
        compiler_params=pltpu.CompilerParams(dimension_semantics=("parallel", "parallel", "arbitrary"),
                                             vmem_limit_bytes=VMEM_LIMIT),
        name="mla",
    )(qmT, km, vmT)


def _moba_kernel(qT_ref, k_ref, vT_ref, kmean_ref, bias_ref, o_ref, pen_ref, *, nfar):
    blk = MOBA_BLOCK
    qb = MOBA_QBLOCKS
    per_head = qb * blk
    wide = H_MOBA * per_head
    j0 = pl.program_id(1) * qb
    nblk = kmean_ref.shape[0]
    qT = qT_ref[...]
    rhead = lax.broadcasted_iota(jnp.int32, (D_MOBA, per_head), 0) // HEAD_DIM
    qsT = jnp.concatenate([jnp.where(rhead == h, qT, 0.0) for h in range(H_MOBA)], axis=1)

    gate = jnp.dot(kmean_ref[:, 0, :], qsT, preferred_element_type=F32, precision=lax.Precision.HIGHEST)
    bidx = lax.broadcasted_iota(jnp.int32, (nblk, wide), 0)
    own = j0 + (lax.broadcasted_iota(jnp.int32, (nblk, wide), 1) % per_head) // blk
    gate = jnp.where(bidx < own, gate, -jnp.inf)
    pen = jnp.full((nblk, wide), 2 * NEG, F32)
    for _ in range(MOBA_TOPK):
        mx = jnp.max(gate, axis=0, keepdims=True)
        first = jnp.min(jnp.where(gate == mx, bidx, nblk), axis=0, keepdims=True)
        hit = bidx == first
        pen = jnp.where(hit & (mx > -jnp.inf), 0.0, pen)
        gate = jnp.where(hit, -jnp.inf, gate)

    pen_ref[...] = jnp.where(bidx == own, 0.0, pen)
    rhs = qsT.astype(BF16)

    def scores(n):
        s = _dot(k_ref[pl.ds(pl.multiple_of(n * blk, blk), blk), :], rhs)
        tiles = [bias_ref[jnp.clip(j0 + c - n, 0, nfar)] for c in range(qb)]
        bias = jnp.concatenate([tiles[c][:, h * blk:(h + 1) * blk] for h in range(H_MOBA) for c in range(qb)],
                               axis=1)
        return s + bias

    ones = jnp.ones((ONES_ROWS, blk), BF16)

    def weighted(n, p):
        vT = vT_ref[n]
        return jnp.concatenate(
            [_dot(jnp.concatenate([vT[h * HEAD_DIM:(h + 1) * HEAD_DIM], ones], axis=0),
                  p[:, h * per_head:(h + 1) * per_head]) for h in range(H_MOBA)], axis=1)

    def pair(t, carry):
        m, acc = carry
        n0 = j0 + qb - 1 - 2 * t
        n1 = n0 - 1
        s0, s1 = scores(n0), scores(n1)
        for n, s in ((n0, s0), (n1, s1)):
            pen = pen_ref[pl.ds(n, 1), :]
            m_new = jnp.maximum(m, jnp.max(s, axis=0, keepdims=True) + pen)
            p = jnp.exp2(s - (m_new - pen)).astype(BF16)
            acc = jnp.exp2(m - m_new) * acc + weighted(n, p)
            m = m_new
        return m, acc

    init = (jnp.full((1, wide), NEG, F32), jnp.zeros((HEAD_DIM + ONES_ROWS, wide), F32))
    _, acc = lax.fori_loop(0, (j0 + qb) // 2, pair, init)
    oT = acc[:HEAD_DIM] / acc[HEAD_DIM:HEAD_DIM + 1]
    o_ref[...] = jnp.concatenate([oT[:, h * per_head:(h + 1) * per_head] for h in range(H_MOBA)], axis=0).T


def _moba(qcT, kc, vcT, kmean, bias, batch, seq):
    t = kc.shape[0]
    nblk = seq // MOBA_BLOCK
    nstep = nblk // MOBA_QBLOCKS
    rows = MOBA_QBLOCKS * MOBA_BLOCK
    nfar = bias.shape[0] - 1
    assert MOBA_QBLOCKS % 2 == 0 and nblk % MOBA_QBLOCKS == 0
    return pl.pallas_call(
        functools.partial(_moba_kernel, nfar=nfar),
        grid=(batch, nstep),
        in_specs=[pl.BlockSpec((D_MOBA, rows), lambda b, j: (0, b * nstep + j)),
                  pl.BlockSpec((seq, D_MOBA), lambda b, j: (b, 0)),
                  pl.BlockSpec((nblk, D_MOBA, MOBA_BLOCK), lambda b, j: (b, 0, 0)),
                  pl.BlockSpec((nblk, 1, D_MOBA), lambda b, j: (b, 0, 0)),
                  pl.BlockSpec(bias.shape, lambda b, j: (0, 0, 0), pipeline_mode=pl.Buffered(1))],
        out_specs=pl.BlockSpec((rows, D_MOBA), lambda b, j: (b * nstep + j, 0)),
        out_shape=jax.ShapeDtypeStruct((t, D_MOBA), F32),
        scratch_shapes=[pltpu.VMEM((nblk, H_MOBA * rows), F32)],
        compiler_params=pltpu.CompilerParams(dimension_semantics=("parallel", "arbitrary"),
                                             vmem_limit_bytes=VMEM_LIMIT),
        name="moba",
    )(qcT, kc, vcT, kmean, bias)


def _post_kernel(x_ref, oa_ref, ob_ref, oc_ref, gmix_ref, wo_ref, gmlp_ref, wup_ref, wdn_ref, gfin_ref,
                 y_ref, *, final, ff_chunk):
    gm = gmix_ref[...]
    x = x_ref[...]
    lo = 0
    for o_ref in (oa_ref, ob_ref, oc_ref):
        n = o_ref.shape[1]
        mixed = _rms(o_ref[...], gm[:, lo:lo + n]).astype(BF16)
        x = x + _dot(mixed, wo_ref[lo:lo + n, :])
        lo += n
    hb = _rms(x, gmlp_ref[...]).astype(BF16)
    y_ref[...] = x
    for c in range(wup_ref.shape[1] // ff_chunk):
        cs = slice(c * ff_chunk, (c + 1) * ff_chunk)
        u = jnp.maximum(_dot(hb, wup_ref[:, cs]), 0.0)
        y_ref[...] += _dot((u * u).astype(BF16), wdn_ref[cs, :])
    if final:
        y_ref[...] = _rms(y_ref[...], gfin_ref[...])


def _post(x, oa, ob, oc, layer, gmix, wo, gmlp, wup, wdn, gfin, final, tm=512, ff_chunk=1024):
    t, d = x.shape
    row = lambda i: (i, 0)
    stacked = lambda a: pl.BlockSpec((None,) + a.shape[1:], lambda i: (layer, 0, 0), pipeline_mode=pl.Buffered(1))
    return pl.pallas_call(
        functools.partial(_post_kernel, final=final, ff_chunk=ff_chunk),
        grid=(t // tm,),
        in_specs=[pl.BlockSpec((tm, d), row), pl.BlockSpec((tm, oa.shape[1]), row),
                  pl.BlockSpec((tm, ob.shape[1]), row), pl.BlockSpec((tm, oc.shape[1]), row),
                  stacked(gmix), stacked(wo), stacked(gmlp), stacked(wup), stacked(wdn),
                  pl.BlockSpec(gfin.shape, lambda i: (0, 0))],
        out_specs=pl.BlockSpec((tm, d), row),
        out_shape=jax.ShapeDtypeStruct((t, d), F32),
        compiler_params=pltpu.CompilerParams(dimension_semantics=("parallel",),
                                             vmem_limit_bytes=VMEM_LIMIT),
        name="post",
    )(x, oa, ob, oc, gmix, wo, gmlp, wup, wdn, gfin)


def _rope_tables(seq):
    inv_freq = ROPE_THETA ** (-jnp.arange(0, MLA_ROPE, 2, dtype=F32) / MLA_ROPE)
    ang = jnp.arange(seq, dtype=F32)[:, None] * inv_freq[None, :]
    cos, sin = jnp.cos(ang), jnp.sin(ang)
    half = MLA_ROPE // 2
    one = jnp.ones((seq, MLA_NOPE), F32)
    zero = jnp.zeros((seq, MLA_NOPE), F32)
    zh = jnp.zeros((seq, half), F32)
    tail1 = jnp.ones((seq, LANES - MLA_NOPE - MLA_ROPE), F32)
    tail0 = jnp.zeros((seq, LANES - MLA_NOPE - MLA_ROPE), F32)
    cos_t = jnp.concatenate([one, cos, cos, tail1], axis=1)
    sin_a = jnp.concatenate([zero, -sin, zh, tail0], axis=1)
    sin_b = jnp.concatenate([zero, zh, sin, tail0], axis=1)
    return cos_t, sin_a, sin_b, cos.T, sin.T


def _lookup(tab, bucket):
    col = lambda b: tab[b][(slice(None),) + (None,) * bucket.ndim]
    out = jnp.broadcast_to(col(0), (tab.shape[1],) + bucket.shape)
    for b in range(1, NUM_BUCKETS):
        out = jnp.where(bucket[None] == b, col(b), out)
    return out.astype(F32)


def _toeplitz(v, rows, cols):
    lead = v.shape[:-1]
    period = rows + cols
    u = jnp.pad(v[..., ::-1], [(0, 0)] * len(lead) + [(0, 1)])
    flat = jnp.tile(u, (1,) * len(lead) + (rows,))[..., :rows * (period - 1)]
    return flat.reshape(lead + (rows, period - 1))[..., rows - 1:rows - 1 + cols]


def _dil_bias(bias_tab):
    span = DIL_SPAN
    npat = len(DILATED_PATTERNS)
    diff = jnp.arange(3 * span - 1) - (span - 1)
    in_band = (diff >= 0) & (diff <= span)
    tabs = []
    for _, dil in DILATED_PATTERNS:
        by_diff = jnp.where(in_band[None], _lookup(bias_tab, _bucket(diff * dil)) * LOG2E, NEG)
        tabs.append(_toeplitz(by_diff, span, 2 * span))
    b = jnp.stack(tabs, axis=1)
    b = b.reshape(H_DIL // 2, 2, npat, span, 2 * span).transpose(0, 2, 1, 3, 4)
    return b.reshape(H_DIL // 2, npat, 2 * span, 2 * span)


def _moba_bias(bias_tab, nblk):
    blk = MOBA_BLOCK
    nfar = min(nblk - 1, REL_MAX_DISTANCE // blk + 1)
    dist = jnp.arange((nfar + 2) * blk - 1) - (blk - 1)
    by_dist = jnp.where(dist[None] >= 0, _lookup(bias_tab, _bucket(dist)) * LOG2E, NEG)
    windows = jnp.stack([by_dist[:, d * blk:(d + 2) * blk - 1] for d in range(nfar + 1)], axis=1)
    b = _toeplitz(windows, blk, blk)
    return b.transpose(1, 3, 0, 2).reshape(nfar + 1, blk, H_MOBA * blk)


def _pad_w_in(w):
    sizes = (D_DIL, D_DIL, D_DIL, MLA_Q_LORA, MLA_KV_LORA, MLA_ROPE, D_MOBA, D_MOBA, D_MOBA)
    parts, lo = [], 0
    for n in sizes:
        parts.append(w[..., lo:lo + n])
        lo += n
    zeros = lambda n: jnp.zeros(w.shape[:-1] + (n,), w.dtype)
    parts[5] = jnp.concatenate([zeros(MLA_NOPE), parts[5], zeros(LANES - MLA_NOPE - MLA_ROPE)], axis=-1)
    return jnp.concatenate(parts, axis=-1).astype(BF16)


def _pad_w_uq_t(w):
    depth, r, _ = w.shape
    w = w.reshape(depth, r, H_MLA, MLA_NOPE + MLA_ROPE)
    w = jnp.pad(w, ((0, 0), (0, 0), (0, 0), (0, LANES - MLA_NOPE - MLA_ROPE)))
    return w.reshape(depth, r, H_MLA * LANES).transpose(0, 2, 1).astype(BF16)


def _split_w_ukv(w):
    depth, r, _ = w.shape
    w = w.reshape(depth, r, H_MLA, MLA_NOPE + MLA_V)
    wk = jnp.pad(w[..., :MLA_NOPE], ((0, 0), (0, 0), (0, 0), (0, LANES - MLA_NOPE)))
    wv = w[..., MLA_NOPE:].reshape(depth, r, H_MLA * MLA_V)
    return wk.reshape(depth, r, H_MLA * LANES).astype(BF16), wv.transpose(0, 2, 1).astype(BF16)


def kernel(x, g_attn, w_in, g_q_lora, g_kv_lora, w_uq, w_ukv, rel_bias, g_mix, w_o, g_mlp, w_up, w_down,
           g_final):
    batch, seq, d = x.shape
    depth = w_in.shape[0]
    assert seq % (DIL_SPAN * max(dil for _, dil in DILATED_PATTERNS)) == 0
    assert seq % IN_TILE == 0 and IN_TILE % MLA_TK == 0
    cos_t, sin_a, sin_b, cosT, sinT = _rope_tables(seq)
    bias_dil = _dil_bias(rel_bias[:, :H_DIL])
    bias_moba = _moba_bias(rel_bias[:, H_DIL:], seq // MOBA_BLOCK)
    rows = lambda v: v[:, None, :].astype(F32)
    w = _pad_w_in(w_in)
    wk, wvT = _split_w_ukv(w_ukv)
    in_params = (rows(g_attn), w, rows(g_q_lora), rows(g_kv_lora), _pad_w_uq_t(w_uq), wk, wvT,
                 w[:, :, C_QC:C_QC + D_MOBA].transpose(0, 2, 1), w[:, :, C_VC:C_VC + D_MOBA].transpose(0, 2, 1))
    post_params = (rows(g_mix), w_o.astype(BF16), rows(g_mlp), w_up.astype(BF16), w_down.astype(BF16))
    gfin = g_final.reshape(1, -1).astype(F32)
    xf = x.reshape(batch * seq, d)
    for l in range(depth):
        qa, ka, va, qmT, km, vmT, qcT, kc, vcT, kmean = _in_proj(
            xf, l, *in_params, cos_t, sin_a, sin_b, cosT, sinT, seq, tm=IN_TILE)
        oa = _dilated(qa, ka, va, bias_dil, batch, seq)
        ob = _mla(qmT, km, vmT, batch, seq, tq=MLA_TQ, tk=MLA_TK)
        oc = _moba(qcT, kc, vcT, kmean, bias_moba, batch, seq)
        xf = _post(xf, oa, ob, oc, l, *post_params, gfin, final=(l == depth - 1))
    return xf.reshape(batch, seq, d)
```
